```python
import math
import jax, jax.numpy as jnp
from jax import lax
import numpy as np

D_MODEL = 1024
BATCH = 2
SEQ = 8192
DEPTH = 1
DEC_BATCH = 128
DEC_SEQ = 1
PAST_LEN = 8192
PAGE_SIZE = 128

HEAD_DIM = 64
SB_HEADS = 8
MB_HEADS = 8
SB_WIDTH = SB_HEADS * HEAD_DIM
MB_WIDTH = MB_HEADS * HEAD_DIM
MB_BLOCK = 256
MB_TOPK = 3
Q_BLOCK = 128
ROPE_THETA = 10000.0
NORM_EPS = 1e-6
SCALE = HEAD_DIM ** -0.5
N_BRANCH = 2
IN_WIDTH = 4 * SB_WIDTH + 4 * MB_WIDTH + N_BRANCH * D_MODEL

kernel_name = "stickbreak_moba_gated_hybrid_step"


def _rmsnorm(x, gain):
    xf = x.astype(jnp.float32)
    y = xf * lax.rsqrt(jnp.mean(xf * xf, axis=-1, keepdims=True) + NORM_EPS)
    return (y * gain.astype(jnp.float32)).astype(x.dtype)


def _rope(x, pos):
    half = HEAD_DIM // 2
    inv = ROPE_THETA ** (-jnp.arange(half, dtype=jnp.float32) / half)
    ang = pos.astype(jnp.float32)[:, None] * inv[None, :]
    cos = jnp.cos(ang)[:, None, :]
    sin = jnp.sin(ang)[:, None, :]
    xf = x.astype(jnp.float32)
    x1, x2 = xf[..., :half], xf[..., half:]
    return jnp.concatenate([x1 * cos - x2 * sin, x2 * cos + x1 * sin], axis=-1).astype(x.dtype)


def _project(x, pos, norm_gain, w_in, b_gate, sb_q_gain, sb_k_gain, mb_q_gain, mb_k_gain):
    n, l, _ = x.shape
    h = _rmsnorm(x, norm_gain)
    p = jnp.einsum("nld,de->nle", h, w_in)
    sizes = (SB_WIDTH,) * 4 + (MB_WIDTH,) * 4 + (D_MODEL,) * N_BRANCH
    idx = [int(i) for i in np.cumsum(sizes)[:-1]]
    qa, ka, va, za, qb, kb, vb, zb, ga, gb = jnp.split(p, idx, axis=-1)
    heads = lambda t, nh: t.reshape(n, l, nh, HEAD_DIM)
    qa = _rmsnorm(heads(qa, SB_HEADS), sb_q_gain)
    ka = _rmsnorm(heads(ka, SB_HEADS), sb_k_gain)
    va = heads(va, SB_HEADS)
    qb = _rope(_rmsnorm(heads(qb, MB_HEADS), mb_q_gain), pos)
    kb = _rope(_rmsnorm(heads(kb, MB_HEADS), mb_k_gain), pos)
    vb = heads(vb, MB_HEADS)
    ga = jax.nn.sigmoid(ga + b_gate[:D_MODEL])
    gb = jax.nn.sigmoid(gb + b_gate[D_MODEL:])
    return qa, ka, va, za, qb, kb, vb, zb, ga, gb


def _stick_breaking(q, qpos, k, v):
    z = jnp.einsum("qhd,khd->hqk", q.astype(jnp.float32), k.astype(jnp.float32)) * SCALE
    past = (jnp.arange(k.shape[0])[None, :] < qpos[:, None])[None]
    log_keep = jnp.where(past, jax.nn.log_sigmoid(-z), 0.0)
    later = lax.cumsum(log_keep, axis=2, reverse=True) - log_keep
    w = jnp.where(past, jnp.exp(jax.nn.log_sigmoid(z) + later), 0.0)
    return jnp.einsum("hqk,khd->qhd", w.astype(v.dtype), v)


def _moba(q, qpos, k, v):
    lk, n_h = k.shape[0], k.shape[1]
    lq = q.shape[0]
    n_blk = -(-lk // MB_BLOCK)
    pad = n_blk * MB_BLOCK - lk
    kb = jnp.pad(k, ((0, pad), (0, 0), (0, 0))).reshape(n_blk, MB_BLOCK, n_h, HEAD_DIM)
    vb = jnp.pad(v, ((0, pad), (0, 0), (0, 0))).reshape(n_blk, MB_BLOCK, n_h, HEAD_DIM)
    own = qpos // MB_BLOCK
    q32 = q.astype(jnp.float32)
    k_mean = jnp.mean(kb.astype(jnp.float32), axis=1)
    gate = jnp.einsum("qhd,nhd->qhn", q32, k_mean)
    fully_past = jnp.arange(n_blk)[None, None, :] < own[:, None, None]
    gate = jnp.where(fully_past, gate, -jnp.inf)
    n_sel = min(MB_TOPK, n_blk)
    _, sel = lax.top_k(gate, n_sel)
    sel_ok = sel < own[:, None, None]
    h_idx = jnp.arange(n_h)[None, :, None]
    k_sel = jnp.transpose(kb, (2, 0, 1, 3))[h_idx, sel]
    v_sel = jnp.transpose(vb, (2, 0, 1, 3))[h_idx, sel]
    k_own = kb[own]
    v_own = vb[own]
    s_sel = jnp.einsum("qhd,qhnbd->qhnb", q32, k_sel.astype(jnp.float32)) * SCALE
    s_sel = jnp.where(sel_ok[..., None], s_sel, -jnp.inf).reshape(lq, n_h, n_sel * MB_BLOCK)
    own_pos = own[:, None] * MB_BLOCK + jnp.arange(MB_BLOCK)[None, :]
    s_own = jnp.einsum("qhd,qbhd->qhb", q32, k_own.astype(jnp.float32)) * SCALE
    s_own = jnp.where((own_pos <= qpos[:, None])[:, None, :], s_own, -jnp.inf)
    p = jax.nn.softmax(jnp.concatenate([s_sel, s_own], axis=-1), axis=-1).astype(v.dtype)
    p_sel = p[..., : n_sel * MB_BLOCK].reshape(lq, n_h, n_sel, MB_BLOCK)
    p_own = p[..., n_sel * MB_BLOCK:]
    return (jnp.einsum("qhnb,qhnbd->qhd", p_sel, v_sel)
            + jnp.einsum("qhb,qbhd->qhd", p_own, v_own))


def _mix(qa, ka, va, qb, kb, vb, qpos):
    return _stick_breaking(qa, qpos, ka, va), _moba(qb, qpos, kb, vb)


def _prompt_mix(qa, ka, va, qb, kb, vb):
    n_b, seq = qa.shape[0], qa.shape[1]
    n_qblk = seq // Q_BLOCK

    def step(i):
        b = i // n_qblk
        t0 = (i % n_qblk) * Q_BLOCK
        qpos = t0 + jnp.arange(Q_BLOCK)
        blk = lambda t: lax.dynamic_slice_in_dim(t[b], t0, Q_BLOCK, axis=0)
        return _mix(blk(qa), ka[b], va[b], blk(qb), kb[b], vb[b], qpos)

    oa, ob = lax.map(step, jnp.arange(n_b * n_qblk))
    return (oa.reshape(n_b, seq, SB_HEADS, HEAD_DIM), ob.reshape(n_b, seq, MB_HEADS, HEAD_DIM))


def _sample_mix(qa, ka, va, qb, kb, vb, page_table, cache_sb_k, cache_sb_v, cache_mb_k, cache_mb_v, layer):
    n_pages = page_table.shape[1]
    past_len = n_pages * PAGE_SIZE
    qpos = past_len + jnp.arange(qa.shape[1])

    def past(cache, pages):
        return cache[layer, pages].reshape(past_len, cache.shape[-2], HEAD_DIM)

    def step(args):
        pages, qa_i, ka_i, va_i, qb_i, kb_i, vb_i = args
        cat = lambda c, new: jnp.concatenate([past(c, pages), new], axis=0)
        return _mix(qa_i, cat(cache_sb_k, ka_i), cat(cache_sb_v, va_i),
                    qb_i, cat(cache_mb_k, kb_i), cat(cache_mb_v, vb_i), qpos)

    return lax.map(step, (page_table, qa, ka, va, qb, kb, vb))


def _merge(x, oa, ob, za, zb, ga, gb, w_branch_sb, w_branch_mb, w_out):
    n, l, _ = x.shape
    ya = (oa.reshape(n, l, SB_WIDTH) * jax.nn.silu(za)) @ w_branch_sb
    yb = (ob.reshape(n, l, MB_WIDTH) * jax.nn.silu(zb)) @ w_branch_mb
    return x + (ga * ya + gb * yb) @ w_out


def setup_inputs(seed: int = 0) -> dict:
    key = jax.random.key(seed)
    ks = jax.random.split(key, 20)
    f32 = jnp.float32
    n_pages = PAST_LEN // PAGE_SIZE
    n_used = DEC_BATCH * n_pages
    n_pool = n_used + n_used // 4
    normal = lambda k, shape, s=1.0: jax.random.normal(k, shape, f32) * s
    pool_shape_sb = (DEPTH, n_pool, PAGE_SIZE, SB_HEADS, HEAD_DIM)
    pool_shape_mb = (DEPTH, n_pool, PAGE_SIZE, MB_HEADS, HEAD_DIM)
    page_table = jax.random.permutation(ks[6], n_pool)[:n_used].reshape(DEC_BATCH, n_pages).astype(jnp.int32)
    return {
        "x_prompt": normal(ks[0], (BATCH, SEQ, D_MODEL)),
        "x_sample": normal(ks[1], (DEC_BATCH, DEC_SEQ, D_MODEL)),
        "cache_sb_k": normal(ks[2], pool_shape_sb),
        "cache_sb_v": normal(ks[3], pool_shape_sb),
        "cache_mb_k": normal(ks[4], pool_shape_mb),
        "cache_mb_v": normal(ks[5], pool_shape_mb),
        "page_table": page_table,
        "norm_gain": 1.0 + normal(ks[7], (DEPTH, D_MODEL), 0.02),
        "w_in": normal(ks[8], (DEPTH, D_MODEL, IN_WIDTH), D_MODEL ** -0.5),
        "b_gate": normal(ks[9], (DEPTH, N_BRANCH * D_MODEL), 0.01),
        "sb_q_gain": 1.0 + normal(ks[10], (DEPTH, HEAD_DIM), 0.02),
        "sb_k_gain": 1.0 + normal(ks[11], (DEPTH, HEAD_DIM), 0.02),
        "mb_q_gain": 1.0 + normal(ks[12], (DEPTH, HEAD_DIM), 0.02),
        "mb_k_gain": 1.0 + normal(ks[13], (DEPTH, HEAD_DIM), 0.02),
        "w_branch_sb": normal(ks[14], (DEPTH, SB_WIDTH, D_MODEL), SB_WIDTH ** -0.5),
        "w_branch_mb": normal(ks[15], (DEPTH, MB_WIDTH, D_MODEL), MB_WIDTH ** -0.5),
        "w_out": normal(ks[16], (DEPTH, D_MODEL, D_MODEL), D_MODEL ** -0.5),
    }


def reference(x_prompt, x_sample, cache_sb_k, cache_sb_v, cache_mb_k, cache_mb_v, page_table,
              norm_gain, w_in, b_gate, sb_q_gain, sb_k_gain, mb_q_gain, mb_k_gain,
              w_branch_sb, w_branch_mb, w_out):
    pos_p = jnp.arange(x_prompt.shape[1])
    pos_s = PAST_LEN + jnp.arange(x_sample.shape[1])
    xp, xs = x_prompt, x_sample
    sbk_p, sbv_p, mbk_p, mbv_p = [], [], [], []
    sbk_s, sbv_s, mbk_s, mbv_s = [], [], [], []
    for l in range(DEPTH):
        lw = (norm_gain[l], w_in[l], b_gate[l], sb_q_gain[l], sb_k_gain[l], mb_q_gain[l], mb_k_gain[l])
        ow = (w_branch_sb[l], w_branch_mb[l], w_out[l])
        qa, ka, va, za, qb, kb, vb, zb, ga, gb = _project(xp, pos_p, *lw)
        oa, ob = _prompt_mix(qa, ka, va, qb, kb, vb)
        xp = _merge(xp, oa, ob, za, zb, ga, gb, *ow)
        sbk_p.append(ka); sbv_p.append(va); mbk_p.append(kb); mbv_p.append(vb)
        qa, ka, va, za, qb, kb, vb, zb, ga, gb = _project(xs, pos_s, *lw)
        oa, ob = _sample_mix(qa, ka, va, qb, kb, vb, page_table,
                             cache_sb_k, cache_sb_v, cache_mb_k, cache_mb_v, l)
        xs = _merge(xs, oa, ob, za, zb, ga, gb, *ow)
        sbk_s.append(ka); sbv_s.append(va); mbk_s.append(kb); mbv_s.append(vb)
    return (xp, xs,
            jnp.stack(sbk_p), jnp.stack(sbv_p), jnp.stack(mbk_p), jnp.stack(mbv_p),
            jnp.stack(sbk_s), jnp.stack(sbv_s), jnp.stack(mbk_s), jnp.stack(mbv_s))
```

```python
import functools

import jax
import jax.numpy as jnp
from jax import lax
from jax.experimental import pallas as pl
from jax.experimental.pallas import tpu as pltpu

HEAD_DIM = 64
N_HEADS = 8
WIDTH = N_HEADS * HEAD_DIM
HALF = HEAD_DIM // 2
MB_BLOCK = 256
MB_TOPK = 3
ROPE_THETA = 10000.0
NORM_EPS = 1e-6
SCALE = HEAD_DIM ** -0.5

LANES = 128
ATT_BLOCK = 256
PROJ_ROWS = 256
MERGE_ROWS = 512
GATE_CHUNK_PAGES = 16
VMEM_LIMIT = 56 * 1024 * 1024

SB_ZERO_LOG = -104.0
NEG_BIG = -1e30

F32 = jnp.float32
BF16 = jnp.bfloat16
NT = (((1,), (1,)), ((), ()))


def _tile_lanes(a, reps):
    return a if reps == 1 else jnp.concatenate([a] * reps, axis=1)


def _split3(x):
    h1 = x.astype(BF16)
    r1 = x - h1.astype(F32)
    h2 = r1.astype(BF16)
    h3 = (r1 - h2.astype(F32)).astype(BF16)
    return h1, h2, h3


def _log_sigmoids(z):
    t = jnp.log1p(jnp.exp(-jnp.abs(z)))
    return jnp.minimum(z, 0.0) - t, -jnp.maximum(z, 0.0) - t


def _head_rows(q_row):
    shape = (N_HEADS, WIDTH)
    head_of_lane = lax.broadcasted_iota(jnp.int32, shape, 1) // HEAD_DIM
    mask = head_of_lane == lax.broadcasted_iota(jnp.int32, shape, 0)
    q32 = jnp.broadcast_to(q_row.astype(F32), shape)
    return jnp.where(mask, q32, 0.0).astype(q_row.dtype), mask


def _proj_kernel(x_ref, ng_ref, wt_ref, bg_ref, gains_ref, cos_ref, sin_ref, *outs, tm, prompt):
    (qa_ref, ka_ref, va_ref, sza_ref, qb_ref, kb_ref, vb_ref, szb_ref, ga_ref, gb_ref) = outs[:10]
    reps = tm // LANES
    x = x_ref[...]
    ms = jnp.mean(x * x, axis=-1, keepdims=True)
    h = (x * lax.rsqrt(ms + NORM_EPS) * ng_ref[...]).astype(BF16)

    def seg(j, rows=WIDTH):
        return lax.dot_general(wt_ref[j * WIDTH:j * WIDTH + rows, :], h, NT,
                               preferred_element_type=F32)

    def head_norm(p, gi):
        p3 = p.reshape(N_HEADS, HEAD_DIM, tm)
        hms = jnp.mean(p3 * p3, axis=1, keepdims=True)
        g = _tile_lanes(gains_ref[gi], reps).reshape(N_HEADS, HEAD_DIM, tm)
        return p3 * lax.rsqrt(hms + NORM_EPS) * g

    def rope(y3):
        c = cos_ref[...][None]
        s = sin_ref[...][None]
        x1, x2 = y3[:, :HALF], y3[:, HALF:]
        return jnp.concatenate([x1 * c - x2 * s, x2 * c + x1 * s], axis=1)

    def silu(z):
        return z * jax.nn.sigmoid(z)

    qa = (head_norm(seg(0), 0) * SCALE).reshape(WIDTH, tm)
    ka = head_norm(seg(1), 1).reshape(WIDTH, tm)
    va = seg(2)
    qb = (rope(head_norm(seg(4), 2)) * SCALE).reshape(WIDTH, tm)
    kb = rope(head_norm(seg(5), 3)).reshape(WIDTH, tm)
    vb = seg(6)
    qa_ref[0] = qa.astype(BF16)
    ka_ref[0] = ka
    va_ref[0] = va
    sza_ref[0] = silu(seg(3))
    qb_ref[0] = qb.astype(BF16)
    kb_ref[0] = kb
    vb_ref[0] = vb
    szb_ref[0] = silu(seg(7))
    d_model = ga_ref.shape[1]
    bg = _tile_lanes(bg_ref[...], reps)
    ga_ref[0] = jax.nn.sigmoid(seg(8, d_model) + bg[:d_model])
    gb_ref[0] = jax.nn.sigmoid(seg(8 + d_model // WIDTH, d_model) + bg[d_model:])
    if prompt:
        ka_row_ref, va_bf_ref, kb_row_ref, vb_bf_ref, kmean_ref = outs[10:]
        ka_row_ref[...] = ka.T.astype(BF16)
        va_bf_ref[0] = va.astype(BF16)
        kb_row = kb.T
        kb_row_ref[...] = kb_row.astype(BF16)
        vb_bf_ref[0] = vb.astype(BF16)
        kmean_ref[...] = jnp.broadcast_to(jnp.mean(kb_row, axis=0, keepdims=True), kmean_ref.shape)
    else:
        qa_row_ref, qb_row_ref, kb_row_ref, vb_row_ref = outs[10:]
        qa_row_ref[...] = qa.T.astype(BF16)
        qb_row_ref[...] = qb.T.astype(BF16)
        kb_row_ref[...] = kb.T.astype(BF16)
        vb_row_ref[...] = vb.T.astype(BF16)


def _project(x, cos_t, sin_t, ng, wt, bg, gains, *, tm, prompt):
    n_b, seq, d_model = x.shape
    rows = n_b * seq
    nt = seq // tm
    in_width = wt.shape[0]
    x2 = x.reshape(rows, d_model)
    fm = lambda f, dt: jax.ShapeDtypeStruct((n_b, f, seq), dt)
    fm_spec = lambda f: pl.BlockSpec((1, f, tm), lambda i: (i // nt, 0, i % nt))
    row_spec = pl.BlockSpec((tm, WIDTH), lambda i: (i, 0))
    const = lambda shape: pl.BlockSpec(shape, lambda i: (0,) * len(shape))
    out_shape = [fm(WIDTH, BF16), fm(WIDTH, F32), fm(WIDTH, F32), fm(WIDTH, F32),
                 fm(WIDTH, BF16), fm(WIDTH, F32), fm(WIDTH, F32), fm(WIDTH, F32),
                 fm(d_model, F32), fm(d_model, F32)]
    out_specs = [fm_spec(WIDTH)] * 8 + [fm_spec(d_model)] * 2
    row_bf = jax.ShapeDtypeStruct((rows, WIDTH), BF16)
    if prompt:
        assert tm == MB_BLOCK
        out_shape += [row_bf, fm(WIDTH, BF16), row_bf, fm(WIDTH, BF16),
                      jax.ShapeDtypeStruct((rows // tm * 8, WIDTH), F32)]
        out_specs += [row_spec, fm_spec(WIDTH), row_spec, fm_spec(WIDTH),
                      pl.BlockSpec((8, WIDTH), lambda i: (i, 0))]
    else:
        out_shape += [row_bf] * 4
        out_specs += [row_spec] * 4
    return pl.pallas_call(
        functools.partial(_proj_kernel, tm=tm, prompt=prompt),
        grid=(rows // tm,),
        in_specs=[pl.BlockSpec((tm, d_model), lambda i: (i, 0)),
                  const((1, d_model)),
                  const((in_width, d_model)),
                  const((2 * d_model, LANES)),
                  const((4, WIDTH, LANES)),
                  pl.BlockSpec((HALF, tm), lambda i: (0, i % nt)),
                  pl.BlockSpec((HALF, tm), lambda i: (0, i % nt))],
        out_specs=out_specs,
        out_shape=out_shape,
        compiler_params=pltpu.CompilerParams(dimension_semantics=("arbitrary",),
                                             vmem_limit_bytes=VMEM_LIMIT),
        name="project_prompt" if prompt else "project_sample",
    )(x2, ng, wt, bg, gains, cos_t, sin_t)


def _stack_heads(qt):
    row = lax.broadcasted_iota(jnp.int32, qt.shape, 0)
    zero = jnp.zeros_like(qt)
    return jnp.concatenate([jnp.where(row < HEAD_DIM, qt, zero),
                            jnp.where(row >= HEAD_DIM, qt, zero)], axis=1)


def _unstack_heads(ot, tq):
    return jnp.concatenate([ot[:HEAD_DIM, :tq], ot[HEAD_DIM:, tq:]], axis=0)


def _sb_prompt_kernel(qt_ref, k_ref, vt_ref, u_ref, ot_ref, acc_ref, carry_ref, *, blk):
    qi = pl.program_id(2)
    qs = _stack_heads(qt_ref[0])
    shape = (blk, 2 * blk)
    key_i = lax.broadcasted_iota(jnp.int32, shape, 0)
    qry_i = lax.broadcasted_iota(jnp.int32, shape, 1) % blk
    acc_ref[...] = jnp.zeros_like(acc_ref)
    carry_ref[...] = jnp.zeros_like(carry_ref)

    def step(j, diag):
        start = pl.multiple_of(j * blk, blk)
        z = jnp.dot(k_ref[pl.ds(start, blk), :], qs, preferred_element_type=F32)
        ls, lk = _log_sigmoids(z)
        if diag:
            past = key_i < qry_i
            lk = jnp.where(past, lk, 0.0)
        u = u_ref[...]
        cum = sum(jnp.dot(u, part, preferred_element_type=F32) for part in _split3(lk))
        w = jnp.exp(ls + cum[:blk] + carry_ref[0:1, :])
        if diag:
            w = jnp.where(past, w, 0.0)
        acc_ref[...] += jnp.dot(vt_ref[0, :, pl.ds(start, blk)], w.astype(BF16),
                                preferred_element_type=F32)
        carry_ref[...] += cum[blk:]

    step(qi, True)

    def cond(s):
        return jnp.logical_and(s[0] >= 0, s[1] >= SB_ZERO_LOG)

    def body(s):
        step(s[0], False)
        return s[0] - 1, jnp.max(carry_ref[...])

    lax.while_loop(cond, body, (qi - 1, jnp.max(carry_ref[...])))
    ot_ref[0] = _unstack_heads(acc_ref[...], blk)


def _sb_prompt(qt, k_row, vt_bf, blk):
    n_b, _, seq = qt.shape
    n_blk = seq // blk
    pair = 2 * HEAD_DIM
    ones_rows = 16
    r = lax.broadcasted_iota(jnp.int32, (blk + ones_rows, blk), 0)
    c = lax.broadcasted_iota(jnp.int32, (blk + ones_rows, blk), 1)
    u = jnp.where(jnp.logical_or(c > r, r >= blk), 1.0, 0.0).astype(BF16)
    return pl.pallas_call(
        functools.partial(_sb_prompt_kernel, blk=blk),
        grid=(n_b, N_HEADS // 2, n_blk),
        in_specs=[pl.BlockSpec((1, pair, blk), lambda b, hp, qi: (b, hp, qi)),
                  pl.BlockSpec((seq, pair), lambda b, hp, qi: (b, hp)),
                  pl.BlockSpec((1, pair, seq), lambda b, hp, qi: (b, hp, 0)),
                  pl.BlockSpec((blk + ones_rows, blk), lambda b, hp, qi: (0, 0))],
        out_specs=pl.BlockSpec((1, pair, blk), lambda b, hp, qi: (b, hp, qi)),
        out_shape=jax.ShapeDtypeStruct((n_b, WIDTH, seq), F32),
        scratch_shapes=[pltpu.VMEM((pair, 2 * blk), F32), pltpu.VMEM((ones_rows, 2 * blk), F32)],
        compiler_params=pltpu.CompilerParams(dimension_semantics=("arbitrary",) * 3,
                                             vmem_limit_bytes=VMEM_LIMIT),
        name="stickbreak_prompt",
    )(qt, k_row, vt_bf, u)


def _top3_rows(gate, n_idx, valid):
    gm = jnp.where(valid, gate, -jnp.inf)
    n_f = n_idx.astype(F32)
    sel = jnp.zeros(gate.shape, jnp.bool_)
    for _ in range(MB_TOPK):
        m = jnp.max(gm, axis=0, keepdims=True)
        idx = jnp.min(jnp.where(gm == m, n_f, 1e9), axis=0, keepdims=True)
        pick = n_f == idx
        sel = jnp.logical_or(sel, pick)
        gm = jnp.where(pick, -jnp.inf, gm)
    return jnp.logical_and(sel, valid)


def _mb_prompt_kernel(qt_ref, k_ref, vt_ref, kmean_ref, ot_ref, acc_ref, m_ref, sel_ref, *, blk, n_blk):
    qi = pl.program_id(2)
    qs = _stack_heads(qt_ref[0])
    ones = jnp.ones((16, blk), BF16)

    km = kmean_ref[...].reshape(n_blk, 8, 2 * HEAD_DIM)[:, 0, :].astype(BF16)
    gate = jnp.dot(km, qs, preferred_element_type=F32)
    n_idx = lax.broadcasted_iota(jnp.int32, gate.shape, 0)
    sel = _top3_rows(gate, n_idx, n_idx < qi)
    sel_ref[...] = jnp.where(sel, 1.0, 0.0)

    def scores(j):
        start = pl.multiple_of(j * blk, blk)
        s = jnp.dot(k_ref[pl.ds(start, blk), :], qs, preferred_element_type=F32)
        vt1 = jnp.concatenate([vt_ref[0, :, pl.ds(start, blk)], ones], axis=0)
        return s, vt1

    s, vt1 = scores(qi)
    shape = (blk, 2 * blk)
    causal = lax.broadcasted_iota(jnp.int32, shape, 0) <= lax.broadcasted_iota(jnp.int32, shape, 1) % blk
    s = jnp.where(causal, s, -jnp.inf)
    m0 = jnp.max(s, axis=0, keepdims=True)
    m_ref[...] = m0
    acc_ref[...] = jnp.dot(vt1, jnp.exp(s - m0).astype(BF16), preferred_element_type=F32)

    def body(j, carry):
        s, vt1 = scores(j)
        is_sel = sel_ref[pl.ds(j, 1), :] > 0.5
        m_old = m_ref[...]
        m_new = jnp.where(is_sel, jnp.maximum(m_old, jnp.max(s, axis=0, keepdims=True)), m_old)
        p = jnp.exp(s - jnp.where(is_sel, m_new, jnp.inf))
        acc_ref[...] = acc_ref[...] * jnp.exp(m_old - m_new) + jnp.dot(
            vt1, p.astype(BF16), preferred_element_type=F32)
        m_ref[...] = m_new
        return carry

    lax.fori_loop(0, qi, body, 0)
    pair = 2 * HEAD_DIM
    acc = acc_ref[...]
    ot_ref[0] = _unstack_heads(acc[:pair] / acc[pair:pair + 1], blk)


def _mb_prompt(qt, k_row, vt_bf, kmean, blk):
    n_b, _, seq = qt.shape
    n_blk = seq // blk
    pair = 2 * HEAD_DIM
    return pl.pallas_call(
        functools.partial(_mb_prompt_kernel, blk=blk, n_blk=n_blk),
        grid=(n_b, N_HEADS // 2, n_blk),
        in_specs=[pl.BlockSpec((1, pair, blk), lambda b, hp, qi: (b, hp, qi)),
                  pl.BlockSpec((seq, pair), lambda b, hp, qi: (b, hp)),
                  pl.BlockSpec((1, pair, seq), lambda b, hp, qi: (b, hp, 0)),
                  pl.BlockSpec((n_blk * 8, pair), lambda b, hp, qi: (b, hp))],
        out_specs=pl.BlockSpec((1, pair, blk), lambda b, hp, qi: (b, hp, qi)),
        out_shape=jax.ShapeDtypeStruct((n_b, WIDTH, seq), F32),
        scratch_shapes=[pltpu.VMEM((pair + 16, 2 * blk), F32), pltpu.VMEM((1, 2 * blk), F32),
                        pltpu.VMEM((n_blk, 2 * blk), F32)],
        compiler_params=pltpu.CompilerParams(dimension_semantics=("arbitrary",) * 3,
                                             vmem_limit_bytes=VMEM_LIMIT),
        name="moba_prompt",
    )(qt, k_row, vt_bf, kmean)


def _merge_kernel(oa_ref, ob_ref, sza_ref, szb_ref, ga_ref, gb_ref, x_ref,
                  wsb_ref, wmb_ref, wout_ref, y_ref):
    ua = (oa_ref[0] * sza_ref[0]).astype(BF16)
    ub = (ob_ref[0] * szb_ref[0]).astype(BF16)
    ya = jnp.dot(wsb_ref[...], ua, preferred_element_type=F32)
    yb = jnp.dot(wmb_ref[...], ub, preferred_element_type=F32)
    mix = (ga_ref[0] * ya + gb_ref[0] * yb).astype(BF16)
    yo = jnp.dot(wout_ref[...], mix, preferred_element_type=F32)
    y_ref[...] = x_ref[...] + yo.T


def _merge(x, oa, ob, sza, szb, ga, gb, wsb_t, wmb_t, wout_t, tm):
    n_b, seq, d_model = x.shape
    rows = n_b * seq
    nt = seq // tm
    fm_spec = lambda f: pl.BlockSpec((1, f, tm), lambda i: (i // nt, 0, i % nt))
    const = lambda shape: pl.BlockSpec(shape, lambda i: (0,) * len(shape))
    y = pl.pallas_call(
        _merge_kernel,
        grid=(rows // tm,),
        in_specs=[fm_spec(WIDTH)] * 4 + [fm_spec(d_model)] * 2
                 + [pl.BlockSpec((tm, d_model), lambda i: (i, 0)),
                    const((d_model, WIDTH)), const((d_model, WIDTH)), const((d_model, d_model))],
        out_specs=pl.BlockSpec((tm, d_model), lambda i: (i, 0)),
        out_shape=jax.ShapeDtypeStruct((rows, d_model), F32),
        compiler_params=pltpu.CompilerParams(dimension_semantics=("arbitrary",),
                                             vmem_limit_bytes=VMEM_LIMIT),
        name="merge",
    )(oa, ob, sza, szb, ga, gb, x.reshape(rows, d_model), wsb_t, wmb_t, wout_t)
    return y.reshape(n_b, seq, d_model)


def _sb_decode_kernel(pt_ref, q_ref, k1_ref, v1_ref, k0_ref, v0_ref, ut_ref, kpool_ref, vpool_ref,
                      o_ref, acc_ref, carry_ref, kbuf, vbuf, sem, *, n_pages):
    b = pl.program_id(0)
    qmat, head_mask = _head_rows(q_ref[0])
    acc_ref[...] = jnp.zeros_like(acc_ref)
    carry_ref[...] = jnp.zeros_like(carry_ref)
    page_sz = ut_ref.shape[0]

    def process(kt, vt):
        z = jnp.dot(qmat, kt.astype(BF16), preferred_element_type=F32)
        ls, lk = _log_sigmoids(z)
        ut = ut_ref[...]
        cum = sum(jnp.dot(part, ut, preferred_element_type=F32) for part in _split3(lk))
        w = jnp.exp(ls + cum[:, :page_sz] + carry_ref[...])
        acc_ref[...] += lax.dot_general(w.astype(BF16), vt.astype(BF16), NT,
                                        preferred_element_type=F32)
        carry_ref[...] += cum[:, page_sz:]

    process(k1_ref[0], v1_ref[0])
    process(k0_ref[0], v0_ref[0])

    def cond(s):
        return jnp.logical_and(s[0] >= 0, s[1] >= SB_ZERO_LOG)

    def body(s):
        page = pt_ref[b * n_pages + s[0]]
        ck = pltpu.make_async_copy(kpool_ref.at[page], kbuf, sem.at[0])
        cv = pltpu.make_async_copy(vpool_ref.at[page], vbuf, sem.at[1])
        ck.start()
        cv.start()
        ck.wait()
        cv.wait()
        process(kbuf[...], vbuf[...])
        return s[0] - 1, jnp.max(carry_ref[...])

    lax.while_loop(cond, body, (n_pages - 3, jnp.max(carry_ref[...])))
    o_ref[0] = jnp.sum(jnp.where(head_mask, acc_ref[...], 0.0), axis=0, keepdims=True)


def _sb_decode(q_row, kpool, vpool, pt_flat, n_pages):
    n_seq = q_row.shape[0]
    page_sz = kpool.shape[-1]
    assert n_pages >= 2
    r = lax.broadcasted_iota(jnp.int32, (page_sz, 2 * page_sz), 0)
    c = lax.broadcasted_iota(jnp.int32, (page_sz, 2 * page_sz), 1)
    ut = jnp.where(jnp.logical_or(r > c, c >= page_sz), 1.0, 0.0).astype(BF16)
    page_spec = lambda back: pl.BlockSpec(
        (1, WIDTH, page_sz), lambda b, pt: (pt[b * n_pages + n_pages - back], 0, 0))
    grid_spec = pltpu.PrefetchScalarGridSpec(
        num_scalar_prefetch=1,
        grid=(n_seq,),
        in_specs=[pl.BlockSpec((1, 1, WIDTH), lambda b, pt: (b, 0, 0)),
                  page_spec(1), page_spec(1), page_spec(2), page_spec(2),
                  pl.BlockSpec((page_sz, 2 * page_sz), lambda b, pt: (0, 0)),
                  pl.BlockSpec(memory_space=pl.ANY), pl.BlockSpec(memory_space=pl.ANY)],
        out_specs=pl.BlockSpec((1, 1, WIDTH), lambda b, pt: (b, 0, 0)),
        scratch_shapes=[pltpu.VMEM((N_HEADS, WIDTH), F32), pltpu.VMEM((N_HEADS, page_sz), F32),
                        pltpu.VMEM((WIDTH, page_sz), F32), pltpu.VMEM((WIDTH, page_sz), F32),
                        pltpu.SemaphoreType.DMA((2,))])
    return pl.pallas_call(
        functools.partial(_sb_decode_kernel, n_pages=n_pages),
        grid_spec=grid_spec,
        out_shape=jax.ShapeDtypeStruct((n_seq, 1, WIDTH), F32),
        compiler_params=pltpu.CompilerParams(dimension_semantics=("arbitrary",),
                                             vmem_limit_bytes=VMEM_LIMIT),
        name="stickbreak_decode",
    )(pt_flat, q_row.reshape(n_seq, 1, WIDTH), kpool, vpool, kpool, vpool, ut, kpool, vpool)


def _mb_scores_kernel(pt_ref, q_ref, knew_ref, kpool_ref, z_ref, sel_ref, buf, sem, gate_ref,
                      *, n_pages, chunk):
    b = pl.program_id(0)
    c = pl.program_id(1)
    n_chunks = pl.num_programs(1)
    t = b * n_chunks + c
    total = pl.num_programs(0) * n_chunks
    page_sz = buf.shape[-1]

    def copies(tt, slot):
        base = (tt // n_chunks) * n_pages + (tt % n_chunks) * chunk
        return [pltpu.make_async_copy(kpool_ref.at[pt_ref[base + i]], buf.at[slot, i], sem.at[slot])
                for i in range(chunk)]

    @pl.when(t == 0)
    def _():
        for cp in copies(t, 0):
            cp.start()

    @pl.when(t + 1 < total)
    def _():
        for cp in copies(t + 1, (t + 1) % 2):
            cp.start()

    slot = t % 2
    for cp in copies(t, slot):
        cp.wait()

    qmat, _ = _head_rows(q_ref[0])
    lane = lax.broadcasted_iota(jnp.int32, (N_HEADS, page_sz), 1)

    @pl.when(c == 0)
    def _():
        gate_ref[...] = jnp.full(gate_ref.shape, -jnp.inf, F32)

    pages_per_blk = MB_BLOCK // page_sz
    gate = gate_ref[...]
    zsum = None
    for i in range(chunk):
        z = jnp.dot(qmat, buf[slot, i].astype(BF16), preferred_element_type=F32)
        z_ref[0, c * chunk + i] = z
        zsum = z if i % pages_per_blk == 0 else zsum + z
        if i % pages_per_blk == pages_per_blk - 1:
            blk_idx = c * (chunk // pages_per_blk) + i // pages_per_blk
            g = jnp.sum(zsum, axis=1, keepdims=True) * (1.0 / MB_BLOCK)
            gate = jnp.where(lane == blk_idx, g, gate)
    gate_ref[...] = gate

    @pl.when(c == n_chunks - 1)
    def _():
        knew = jnp.broadcast_to(knew_ref[0], (page_sz, WIDTH))
        z_ref[0, n_pages] = lax.dot_general(qmat, knew, NT, preferred_element_type=F32)
        gm = gate_ref[...]
        lane_f = lane.astype(F32)
        selv = jnp.zeros((N_HEADS, page_sz), jnp.int32)
        for it in range(MB_TOPK):
            m = jnp.max(gm, axis=1, keepdims=True)
            idx = jnp.min(jnp.where(gm == m, lane_f, 1e9), axis=1, keepdims=True)
            selv = jnp.where(lane == it, idx.astype(jnp.int32), selv)
            gm = jnp.where(lane_f == idx, -jnp.inf, gm)
        sel_ref[0] = selv


def _mb_scores(q_row, knew_row, kpool, pt_flat, n_pages):
    n_seq = q_row.shape[0]
    page_sz = kpool.shape[-1]
    chunk = min(GATE_CHUNK_PAGES, n_pages)
    assert n_pages % chunk == 0 and chunk % (MB_BLOCK // page_sz) == 0
    assert n_pages * page_sz // MB_BLOCK <= page_sz
    grid_spec = pltpu.PrefetchScalarGridSpec(
        num_scalar_prefetch=1,
        grid=(n_seq, n_pages // chunk),
        in_specs=[pl.BlockSpec((1, 1, WIDTH), lambda b, c, pt: (b, 0, 0)),
                  pl.BlockSpec((1, 1, WIDTH), lambda b, c, pt: (b, 0, 0)),
                  pl.BlockSpec(memory_space=pl.ANY)],
        out_specs=[pl.BlockSpec((1, n_pages + 1, N_HEADS, page_sz), lambda b, c, pt: (b, 0, 0, 0)),
                   pl.BlockSpec((1, N_HEADS, page_sz), lambda b, c, pt: (b, 0, 0))],
        scratch_shapes=[pltpu.VMEM((2, chunk, WIDTH, page_sz), F32),
                        pltpu.SemaphoreType.DMA((2,)),
                        pltpu.VMEM((N_HEADS, page_sz), F32)])
    return pl.pallas_call(
        functools.partial(_mb_scores_kernel, n_pages=n_pages, chunk=chunk),
        grid_spec=grid_spec,
        out_shape=[jax.ShapeDtypeStruct((n_seq, n_pages + 1, N_HEADS, page_sz), F32),
                   jax.ShapeDtypeStruct((n_seq, N_HEADS, page_sz), jnp.int32)],
        compiler_params=pltpu.CompilerParams(dimension_semantics=("arbitrary",) * 2,
                                             vmem_limit_bytes=VMEM_LIMIT),
        name="moba_decode_scores",
    )(pt_flat, q_row.reshape(n_seq, 1, WIDTH), knew_row.reshape(n_seq, 1, WIDTH), kpool)


def _mb_decode_kernel(pt_ref, sel_ref, z_ref, vnew_ref, vpool_ref, o_ref, vbuf, sem, *, n_pages):
    b = pl.program_id(0)
    n_seq = pl.num_programs(0)
    page_sz = vbuf.shape[-1]
    pages_per_blk = MB_BLOCK // page_sz
    n_chunks = MB_TOPK * pages_per_blk

    def sel_page(bb, h, ci):
        blk = sel_ref[(bb * N_HEADS + h) * MB_TOPK + ci // pages_per_blk]
        return blk * pages_per_blk + ci % pages_per_blk

    def copies(bb, slot):
        return [pltpu.make_async_copy(
                    vpool_ref.at[pt_ref[bb * n_pages + sel_page(bb, h, ci)], pl.ds(h * HEAD_DIM, HEAD_DIM), :],
                    vbuf.at[slot, h, ci], sem.at[slot])
                for h in range(N_HEADS) for ci in range(n_chunks)]

    @pl.when(b == 0)
    def _():
        for cp in copies(b, 0):
            cp.start()

    @pl.when(b + 1 < n_seq)
    def _():
        for cp in copies(b + 1, (b + 1) % 2):
            cp.start()

    slot = b % 2
    for cp in copies(b, slot):
        cp.wait()

    row = lax.broadcasted_iota(jnp.int32, (N_HEADS, page_sz), 0)
    chunks = []
    for ci in range(n_chunks):
        sc = jnp.zeros((N_HEADS, page_sz), F32)
        for h in range(N_HEADS):
            sc = jnp.where(row == h, z_ref[0, sel_page(b, h, ci)], sc)
        chunks.append(sc)
    s = jnp.concatenate(chunks, axis=1)
    s_own = z_ref[0, n_pages]
    m = jnp.maximum(jnp.max(s, axis=1, keepdims=True), s_own[:, :1])
    p = jnp.exp(s - m)
    p_own = jnp.exp(s_own[:, :1] - m)
    denom = jnp.sum(p, axis=1, keepdims=True) + p_own
    vsel = jnp.concatenate(
        [jnp.concatenate([vbuf[slot, h, ci] for ci in range(n_chunks)], axis=1) for h in range(N_HEADS)],
        axis=0)
    res = lax.dot_general(p.astype(BF16), vsel.astype(BF16), NT, preferred_element_type=F32)
    res = res + p_own.astype(BF16).astype(F32) * vnew_ref[0].astype(F32)
    head_mask = (lax.broadcasted_iota(jnp.int32, res.shape, 1) // HEAD_DIM
                 == lax.broadcasted_iota(jnp.int32, res.shape, 0))
    o_ref[0] = jnp.sum(jnp.where(head_mask, res / denom, 0.0), axis=0, keepdims=True)


def _mb_decode(z_all, sel_flat, vnew_row, vpool, pt_flat, n_pages):
    n_seq = z_all.shape[0]
    page_sz = vpool.shape[-1]
    n_chunks = MB_TOPK * (MB_BLOCK // page_sz)
    grid_spec = pltpu.PrefetchScalarGridSpec(
        num_scalar_prefetch=2,
        grid=(n_seq,),
        in_specs=[pl.BlockSpec((1, n_pages + 1, N_HEADS, page_sz), lambda b, pt, sel: (b, 0, 0, 0)),
                  pl.BlockSpec((1, 1, WIDTH), lambda b, pt, sel: (b, 0, 0)),
                  pl.BlockSpec(memory_space=pl.ANY)],
        out_specs=pl.BlockSpec((1, 1, WIDTH), lambda b, pt, sel: (b, 0, 0)),
        scratch_shapes=[pltpu.VMEM((2, N_HEADS, n_chunks, HEAD_DIM, page_sz), F32),
                        pltpu.SemaphoreType.DMA((2,))])
    return pl.pallas_call(
        functools.partial(_mb_decode_kernel, n_pages=n_pages),
        grid_spec=grid_spec,
        out_shape=jax.ShapeDtypeStruct((n_seq, 1, WIDTH), F32),
        compiler_params=pltpu.CompilerParams(dimension_semantics=("arbitrary",),
                                             vmem_limit_bytes=VMEM_LIMIT),
        name="moba_decode",
    )(pt_flat, sel_flat, z_all, vnew_row.reshape(n_seq, 1, WIDTH), vpool)


def _rope_tables(pos):
    inv = ROPE_THETA ** (-jnp.arange(HALF, dtype=F32) / HALF)
    ang = pos.astype(F32)[None, :] * inv[:, None]
    return jnp.cos(ang), jnp.sin(ang)


def _pool_pages(cache, layer):
    n_pool, slots = cache.shape[1], cache.shape[2]
    return jnp.transpose(cache[layer], (0, 2, 3, 1)).reshape(n_pool, WIDTH, slots)


def _heads_out(t):
    n_b, _, seq = t.shape
    return jnp.transpose(t.reshape(n_b, N_HEADS, HEAD_DIM, seq), (0, 3, 1, 2))


def kernel(x_prompt, x_sample, cache_sb_k, cache_sb_v, cache_mb_k, cache_mb_v, page_table,
           norm_gain, w_in, b_gate, sb_q_gain, sb_k_gain, mb_q_gain, mb_k_gain,
           w_branch_sb, w_branch_mb, w_out):
    depth = w_in.shape[0]
    n_seq, dec_seq, d_model = x_sample.shape
    assert dec_seq == 1
    seq = x_prompt.shape[1]
    n_pages = page_table.shape[1]
    page_sz = cache_sb_k.shape[2]
    past_len = n_pages * page_sz
    assert past_len % MB_BLOCK == 0 and past_len // MB_BLOCK >= MB_TOPK
    pt_flat = page_table.reshape(-1)
    cos_p, sin_p = _rope_tables(jnp.arange(seq))
    cos_s, sin_s = _rope_tables(jnp.full((n_seq,), past_len))

    xp, xs = x_prompt, x_sample.reshape(1, n_seq, d_model)
    prompt_kv = [[], [], [], []]
    sample_kv = [[], [], [], []]
    for l in range(depth):
        wt = w_in[l].T.astype(BF16)
        ng = norm_gain[l][None, :]
        bg = jnp.broadcast_to(b_gate[l][:, None], (2 * d_model, LANES))
        gains = jnp.stack([jnp.broadcast_to(jnp.tile(g[l], N_HEADS)[:, None], (WIDTH, LANES))
                           for g in (sb_q_gain, sb_k_gain, mb_q_gain, mb_k_gain)])
        wsb_t = w_branch_sb[l].T.astype(BF16)
        wmb_t = w_branch_mb[l].T.astype(BF16)
        wout_t = w_out[l].T.astype(BF16)

        (qa, ka, va, sza, qb, kb, vb, szb, ga, gb,
         ka_row, va_bf, kb_row, vb_bf, kmean) = _project(
            xp, cos_p, sin_p, ng, wt, bg, gains, tm=PROJ_ROWS, prompt=True)
        oa = _sb_prompt(qa, ka_row, va_bf, ATT_BLOCK)
        ob = _mb_prompt(qb, kb_row, vb_bf, kmean, MB_BLOCK)
        xp = _merge(xp, oa, ob, sza, szb, ga, gb, wsb_t, wmb_t, wout_t, MERGE_ROWS)
        for dst, t in zip(prompt_kv, (ka, va, kb, vb)):
            dst.append(_heads_out(t))

        (qa, ka, va, sza, qb, kb, vb, szb, ga, gb,
         qa_row, qb_row, kb_new, vb_new) = _project(
            xs, cos_s, sin_s, ng, wt, bg, gains, tm=n_seq, prompt=False)
        oa = _sb_decode(qa_row, _pool_pages(cache_sb_k, l), _pool_pages(cache_sb_v, l), pt_flat, n_pages)
        z_all, sel = _mb_scores(qb_row, kb_new, _pool_pages(cache_mb_k, l), pt_flat, n_pages)
        ob = _mb_decode(z_all, sel[:, :, :MB_TOPK].reshape(-1), vb_new,
                        _pool_pages(cache_mb_v, l), pt_flat, n_pages)
        to_fm = lambda o: jnp.transpose(o.reshape(n_seq, WIDTH))[None]
        xs = _merge(xs, to_fm(oa), to_fm(ob), sza, szb, ga, gb, wsb_t, wmb_t, wout_t, n_seq)
        for dst, t in zip(sample_kv, (ka, va, kb, vb)):
            dst.append(_heads_out(t).reshape(n_seq, 1, N_HEADS, HEAD_DIM))

    return (xp, xs.reshape(n_seq, 1, d_model),
            *[jnp.stack(t) for t in prompt_kv], *[jnp.stack(t) for t in sample_kv])
```

```python
import functools

import jax
import jax.numpy as jnp
from jax import lax
from jax.experimental import pallas as pl
from jax.experimental.pallas import tpu as pltpu

HEAD_DIM = 64
N_HEADS = 8
WIDTH = N_HEADS * HEAD_DIM
HALF = HEAD_DIM // 2
MB_BLOCK = 256
MB_TOPK = 3
ROPE_THETA = 10000.0
NORM_EPS = 1e-6
SCALE = HEAD_DIM ** -0.5

LANES = 128
ATT_BLOCK = 256
PROJ_ROWS = 256
MERGE_ROWS = 512
GATE_CHUNK_PAGES = 16
VMEM_LIMIT = 56 * 1024 * 1024

SB_ZERO_LOG = -104.0
NEG_BIG = -1e30
MB_SMALL_SCORE = 40.0
MB_GROUP = 4

F32 = jnp.float32
BF16 = jnp.bfloat16
NT = (((1,), (1,)), ((), ()))


def _tile_lanes(a, reps):
    return a if reps == 1 else jnp.concatenate([a] * reps, axis=1)


def _split3(x):
    h1 = x.astype(BF16)
    r1 = x - h1.astype(F32)
    h2 = r1.astype(BF16)
    h3 = (r1 - h2.astype(F32)).astype(BF16)
    return h1, h2, h3


def _log_sigmoids(z):
    t = jnp.log1p(jnp.exp(-jnp.abs(z)))
    return jnp.minimum(z, 0.0) - t, -jnp.maximum(z, 0.0) - t


def _head_rows(q_row):
    shape = (N_HEADS, WIDTH)
    head_of_lane = lax.broadcasted_iota(jnp.int32, shape, 1) // HEAD_DIM
    mask = head_of_lane == lax.broadcasted_iota(jnp.int32, shape, 0)
    q32 = jnp.broadcast_to(q_row.astype(F32), shape)
    return jnp.where(mask, q32, 0.0).astype(q_row.dtype), mask


def _proj_kernel(x_ref, ng_ref, wt_ref, bg_ref, gains_ref, cos_ref, sin_ref, *outs, tm, prompt):
    (qa_ref, ka_ref, va_ref, sza_ref, qb_ref, kb_ref, vb_ref, szb_ref, ga_ref, gb_ref) = outs[:10]
    reps = tm // LANES
    x = x_ref[...]
    ms = jnp.mean(x * x, axis=-1, keepdims=True)
    h = (x * lax.rsqrt(ms + NORM_EPS) * ng_ref[...]).astype(BF16)

    def seg(j, rows=WIDTH):
        return lax.dot_general(wt_ref[j * WIDTH:j * WIDTH + rows, :], h, NT,
                               preferred_element_type=F32)

    def head_norm(p, gi):
        p3 = p.reshape(N_HEADS, HEAD_DIM, tm)
        hms = jnp.mean(p3 * p3, axis=1, keepdims=True)
        g = _tile_lanes(gains_ref[gi], reps).reshape(N_HEADS, HEAD_DIM, tm)
        return p3 * lax.rsqrt(hms + NORM_EPS) * g

    def rope(y3):
        c = cos_ref[...][None]
        s = sin_ref[...][None]
        x1, x2 = y3[:, :HALF], y3[:, HALF:]
        return jnp.concatenate([x1 * c - x2 * s, x2 * c + x1 * s], axis=1)

    def silu(z):
        return z * jax.nn.sigmoid(z)

    qa = (head_norm(seg(0), 0) * SCALE).reshape(WIDTH, tm)
    ka = head_norm(seg(1), 1).reshape(WIDTH, tm)
    va = seg(2)
    qb = (rope(head_norm(seg(4), 2)) * SCALE).reshape(WIDTH, tm)
    kb = rope(head_norm(seg(5), 3)).reshape(WIDTH, tm)
    vb = seg(6)
    qa_ref[0] = qa.astype(BF16)
    ka_ref[0] = ka
    va_ref[0] = va
    sza_ref[0] = silu(seg(3))
    qb_ref[0] = qb.astype(BF16)
    kb_ref[0] = kb
    vb_ref[0] = vb
    szb_ref[0] = silu(seg(7))
    d_model = ga_ref.shape[1]
    bg = _tile_lanes(bg_ref[...], reps)
    ga_ref[0] = jax.nn.sigmoid(seg(8, d_model) + bg[:d_model])
    gb_ref[0] = jax.nn.sigmoid(seg(8 + d_model // WIDTH, d_model) + bg[d_model:])
    if prompt:
        ka_row_ref, va_bf_ref, kb_row_ref, vb_bf_ref, kmean_ref = outs[10:]
        ka_row_ref[...] = ka.T.astype(BF16)
        va_bf_ref[0] = va.astype(BF16)
        kb_row = kb.T
        kb_row_ref[...] = kb_row.astype(BF16)
        vb_bf_ref[0] = vb.astype(BF16)
        kmean_ref[...] = jnp.broadcast_to(jnp.mean(kb_row, axis=0, keepdims=True), kmean_ref.shape)
    else:
        qa_row_ref, qb_row_ref, kb_row_ref, vb_row_ref = outs[10:]
        qa_row_ref[...] = qa.T.astype(BF16)
        qb_row_ref[...] = qb.T.astype(BF16)
        kb_row_ref[...] = kb.T.astype(BF16)
        vb_row_ref[...] = vb.T.astype(BF16)


def _project(x, cos_t, sin_t, ng, wt, bg, gains, *, tm, prompt):
    n_b, seq, d_model = x.shape
    rows = n_b * seq
    nt = seq // tm
    in_width = wt.shape[0]
    x2 = x.reshape(rows, d_model)
    fm = lambda f, dt: jax.ShapeDtypeStruct((n_b, f, seq), dt)
    fm_spec = lambda f: pl.BlockSpec((1, f, tm), lambda i: (i // nt, 0, i % nt))
    row_spec = pl.BlockSpec((tm, WIDTH), lambda i: (i, 0))
    const = lambda shape: pl.BlockSpec(shape, lambda i: (0,) * len(shape))
    out_shape = [fm(WIDTH, BF16), fm(WIDTH, F32), fm(WIDTH, F32), fm(WIDTH, F32),
                 fm(WIDTH, BF16), fm(WIDTH, F32), fm(WIDTH, F32), fm(WIDTH, F32),
                 fm(d_model, F32), fm(d_model, F32)]
    out_specs = [fm_spec(WIDTH)] * 8 + [fm_spec(d_model)] * 2
    row_bf = jax.ShapeDtypeStruct((rows, WIDTH), BF16)
    if prompt:
        assert tm == MB_BLOCK
        out_shape += [row_bf, fm(WIDTH, BF16), row_bf, fm(WIDTH, BF16),
                      jax.ShapeDtypeStruct((rows // tm * 8, WIDTH), F32)]
        out_specs += [row_spec, fm_spec(WIDTH), row_spec, fm_spec(WIDTH),
                      pl.BlockSpec((8, WIDTH), lambda i: (i, 0))]
    else:
        out_shape += [row_bf] * 4
        out_specs += [row_spec] * 4
    return pl.pallas_call(
        functools.partial(_proj_kernel, tm=tm, prompt=prompt),
        grid=(rows // tm,),
        in_specs=[pl.BlockSpec((tm, d_model), lambda i: (i, 0)),
                  const((1, d_model)),
                  const((in_width, d_model)),
                  const((2 * d_model, LANES)),
                  const((4, WIDTH, LANES)),
                  pl.BlockSpec((HALF, tm), lambda i: (0, i % nt)),
                  pl.BlockSpec((HALF, tm), lambda i: (0, i % nt))],
        out_specs=out_specs,
        out_shape=out_shape,
        compiler_params=pltpu.CompilerParams(dimension_semantics=("arbitrary",),
                                             vmem_limit_bytes=VMEM_LIMIT),
        name="project_prompt" if prompt else "project_sample",
    )(x2, ng, wt, bg, gains, cos_t, sin_t)


def _stack_heads(qt):
    row = lax.broadcasted_iota(jnp.int32, qt.shape, 0)
    zero = jnp.zeros_like(qt)
    return jnp.concatenate([jnp.where(row < HEAD_DIM, qt, zero),
                            jnp.where(row >= HEAD_DIM, qt, zero)], axis=1)


def _unstack_heads(ot, tq):
    return jnp.concatenate([ot[:HEAD_DIM, :tq], ot[HEAD_DIM:, tq:]], axis=0)


def _log_keep(z):
    return -(jnp.maximum(z, 0.0) + jnp.log(1.0 + jnp.exp(-jnp.abs(z))))


def _split2(x):
    hi = x.astype(BF16)
    return hi, (x - hi.astype(F32)).astype(BF16)


def _sb_prompt_kernel(qt_ref, k_ref, vt_ref, u_ref, ot_ref, acc_ref, carry_ref, *, blk):
    qi = pl.program_id(2)
    qs = _stack_heads(qt_ref[0])
    shape = (blk, blk)
    past = lax.broadcasted_iota(jnp.int32, shape, 0) < lax.broadcasted_iota(jnp.int32, shape, 1)
    acc_ref[...] = jnp.zeros_like(acc_ref)
    carry_ref[...] = jnp.zeros_like(carry_ref)

    def step(j, diag):
        start = pl.multiple_of(j * blk, blk)
        k_blk = k_ref[pl.ds(start, blk), :]
        vt_blk = vt_ref[0, :, pl.ds(start, blk)]
        carry = carry_ref[...]
        u = u_ref[...]
        halves = [slice(hd * blk, (hd + 1) * blk) for hd in range(2)]
        zs = [jnp.dot(k_blk, qs[:, cols], preferred_element_type=F32) for cols in halves]
        cums = []
        for z in zs:
            lk = _log_keep(z)
            if diag:
                lk = jnp.where(past, lk, 0.0)
            cums.append(jnp.dot(u, jnp.concatenate(_split2(lk), axis=0), preferred_element_type=F32))
        pv = []
        for z, cum, cols in zip(zs, cums, halves):
            w = jnp.exp(z + cum + carry[:, cols])
            if diag:
                w = jnp.where(past, w, 0.0)
            pv.append(jnp.dot(vt_blk, w.astype(BF16), preferred_element_type=F32))
        acc_ref[...] += jnp.concatenate(pv, axis=1)
        carry_ref[...] = carry + jnp.concatenate([cum[0:1] for cum in cums], axis=1)

    step(qi, True)

    def cond(s):
        return jnp.logical_and(s[0] >= 0, s[1] >= SB_ZERO_LOG)

    def body(s):
        step(s[0], False)
        return s[0] - 1, jnp.max(carry_ref[...])

    lax.while_loop(cond, body, (qi - 1, jnp.max(carry_ref[...])))
    ot_ref[0] = _unstack_heads(acc_ref[...], blk)


def _sb_prompt(qt, k_row, vt_bf, blk):
    n_b, _, seq = qt.shape
    n_blk = seq // blk
    pair = 2 * HEAD_DIM
    r = lax.broadcasted_iota(jnp.int32, (blk, 2 * blk), 0)
    c = lax.broadcasted_iota(jnp.int32, (blk, 2 * blk), 1) % blk
    u = jnp.where(c >= r, 1.0, 0.0).astype(BF16)
    return pl.pallas_call(
        functools.partial(_sb_prompt_kernel, blk=blk),
        grid=(n_b, N_HEADS // 2, n_blk),
        in_specs=[pl.BlockSpec((1, pair, blk), lambda b, hp, qi: (b, hp, qi)),
                  pl.BlockSpec((seq, pair), lambda b, hp, qi: (b, hp)),
                  pl.BlockSpec((1, pair, seq), lambda b, hp, qi: (b, hp, 0)),
                  pl.BlockSpec((blk, 2 * blk), lambda b, hp, qi: (0, 0))],
        out_specs=pl.BlockSpec((1, pair, blk), lambda b, hp, qi: (b, hp, qi)),
        out_shape=jax.ShapeDtypeStruct((n_b, WIDTH, seq), F32),
        scratch_shapes=[pltpu.VMEM((pair, 2 * blk), F32), pltpu.VMEM((1, 2 * blk), F32)],
        compiler_params=pltpu.CompilerParams(dimension_semantics=("arbitrary",) * 3,
                                             vmem_limit_bytes=VMEM_LIMIT),
        name="stickbreak_prompt",
    )(qt, k_row, vt_bf, u)


def _top3_rows(gate, n_idx, valid):
    gm = jnp.where(valid, gate, -jnp.inf)
    n_f = n_idx.astype(F32)
    sel = jnp.zeros(gate.shape, jnp.bool_)
    for _ in range(MB_TOPK):
        m = jnp.max(gm, axis=0, keepdims=True)
        idx = jnp.min(jnp.where(gm == m, n_f, 1e9), axis=0, keepdims=True)
        pick = n_f == idx
        sel = jnp.logical_or(sel, pick)
        gm = jnp.where(pick, -jnp.inf, gm)
    return jnp.logical_and(sel, valid)


def _mb_prompt_kernel(small_ref, qt_ref, k_ref, vt_ref, kmean_ref, ot_ref, acc_ref, m_ref, sel_ref,
                      *, blk, n_blk, grp):
    qi = pl.program_id(2)
    qs = _stack_heads(qt_ref[0])
    ones = jnp.ones((16, blk), BF16)

    km = kmean_ref[...].reshape(n_blk, 8, 2 * HEAD_DIM)[:, 0, :].astype(BF16)
    gate = jnp.dot(km, qs, preferred_element_type=F32)
    n_idx = lax.broadcasted_iota(jnp.int32, gate.shape, 0)
    sel = _top3_rows(gate, n_idx, n_idx < qi)

    def scores(j):
        start = pl.multiple_of(j * blk, blk)
        s = jnp.dot(k_ref[pl.ds(start, blk), :], qs, preferred_element_type=F32)
        vt1 = jnp.concatenate([vt_ref[0, :, pl.ds(start, blk)], ones], axis=0)
        return s, vt1

    s_own, vt1_own = scores(qi)
    shape = (blk, 2 * blk)
    causal = lax.broadcasted_iota(jnp.int32, shape, 0) <= lax.broadcasted_iota(jnp.int32, shape, 1) % blk

    @pl.when(small_ref[0] == 1)
    def _():
        sel_ref[...] = jnp.where(sel, 0.0, NEG_BIG)
        p_own = jnp.where(causal, jnp.exp(s_own), 0.0)
        acc_ref[...] = jnp.dot(vt1_own, p_own.astype(BF16), preferred_element_type=F32)
        ones_g = jnp.ones((16, grp * blk), BF16)

        def body(g, carry):
            start = pl.multiple_of(g * (grp * blk), grp * blk)
            s = jnp.dot(k_ref[pl.ds(start, grp * blk), :], qs, preferred_element_type=F32)
            p = jnp.concatenate(
                [jnp.exp(s[i * blk:(i + 1) * blk] + sel_ref[pl.ds(g * grp + i, 1), :]) for i in range(grp)],
                axis=0).astype(BF16)
            vt1 = jnp.concatenate([vt_ref[0, :, pl.ds(start, grp * blk)], ones_g], axis=0)
            acc_ref[...] += jnp.dot(vt1, p, preferred_element_type=F32)
            return carry

        lax.fori_loop(0, (qi + grp - 1) // grp, body, 0)

    @pl.when(small_ref[0] != 1)
    def _():
        sel_ref[...] = jnp.where(sel, 1.0, 0.0)
        s = jnp.where(causal, s_own, -jnp.inf)
        m0 = jnp.max(s, axis=0, keepdims=True)
        m_ref[...] = m0
        acc_ref[...] = jnp.dot(vt1_own, jnp.exp(s - m0).astype(BF16), preferred_element_type=F32)

        def body(j, carry):
            s, vt1 = scores(j)
            is_sel = sel_ref[pl.ds(j, 1), :] > 0.5
            m_old = m_ref[...]
            m_new = jnp.where(is_sel, jnp.maximum(m_old, jnp.max(s, axis=0, keepdims=True)), m_old)
            p = jnp.exp(s - jnp.where(is_sel, m_new, jnp.inf))
            acc_ref[...] = acc_ref[...] * jnp.exp(m_old - m_new) + jnp.dot(
                vt1, p.astype(BF16), preferred_element_type=F32)
            m_ref[...] = m_new
            return carry

        lax.fori_loop(0, qi, body, 0)

    pair = 2 * HEAD_DIM
    acc = acc_ref[...]
    ot_ref[0] = _unstack_heads(acc[:pair] / acc[pair:pair + 1], blk)


def _mb_prompt(qt, k_row, vt_bf, kmean, small_scores, blk):
    n_b, _, seq = qt.shape
    n_blk = seq // blk
    pair = 2 * HEAD_DIM
    grp = MB_GROUP if n_blk % MB_GROUP == 0 else 1
    grid_spec = pltpu.PrefetchScalarGridSpec(
        num_scalar_prefetch=1,
        grid=(n_b, N_HEADS // 2, n_blk),
        in_specs=[pl.BlockSpec((1, pair, blk), lambda b, hp, qi, f: (b, hp, qi)),
                  pl.BlockSpec((seq, pair), lambda b, hp, qi, f: (b, hp)),
                  pl.BlockSpec((1, pair, seq), lambda b, hp, qi, f: (b, hp, 0)),
                  pl.BlockSpec((n_blk * 8, pair), lambda b, hp, qi, f: (b, hp))],
        out_specs=pl.BlockSpec((1, pair, blk), lambda b, hp, qi, f: (b, hp, qi)),
        scratch_shapes=[pltpu.VMEM((pair + 16, 2 * blk), F32), pltpu.VMEM((1, 2 * blk), F32),
                        pltpu.VMEM((n_blk, 2 * blk), F32)])
    return pl.pallas_call(
        functools.partial(_mb_prompt_kernel, blk=blk, n_blk=n_blk, grp=grp),
        grid_spec=grid_spec,
        out_shape=jax.ShapeDtypeStruct((n_b, WIDTH, seq), F32),
        compiler_params=pltpu.CompilerParams(dimension_semantics=("arbitrary",) * 3,
                                             vmem_limit_bytes=VMEM_LIMIT),
        name="moba_prompt",
    )(small_scores, qt, k_row, vt_bf, kmean)


def _merge_kernel(oa_ref, ob_ref, sza_ref, szb_ref, ga_ref, gb_ref, x_ref,
                  wsb_ref, wmb_ref, wout_ref, y_ref):
    ua = (oa_ref[0] * sza_ref[0]).astype(BF16)
    ub = (ob_ref[0] * szb_ref[0]).astype(BF16)
    ya = jnp.dot(wsb_ref[...], ua, preferred_element_type=F32)
    yb = jnp.dot(wmb_ref[...], ub, preferred_element_type=F32)
    mix = (ga_ref[0] * ya + gb_ref[0] * yb).astype(BF16)
    yo = jnp.dot(wout_ref[...], mix, preferred_element_type=F32)
    y_ref[...] = x_ref[...] + yo.T


def _merge(x, oa, ob, sza, szb, ga, gb, wsb_t, wmb_t, wout_t, tm):
    n_b, seq, d_model = x.shape
    rows = n_b * seq
    nt = seq // tm
    fm_spec = lambda f: pl.BlockSpec((1, f, tm), lambda i: (i // nt, 0, i % nt))
    const = lambda shape: pl.BlockSpec(shape, lambda i: (0,) * len(shape))
    y = pl.pallas_call(
        _merge_kernel,
        grid=(rows // tm,),
        in_specs=[fm_spec(WIDTH)] * 4 + [fm_spec(d_model)] * 2
                 + [pl.BlockSpec((tm, d_model), lambda i: (i, 0)),
                    const((d_model, WIDTH)), const((d_model, WIDTH)), const((d_model, d_model))],
        out_specs=pl.BlockSpec((tm, d_model), lambda i: (i, 0)),
        out_shape=jax.ShapeDtypeStruct((rows, d_model), F32),
        compiler_params=pltpu.CompilerParams(dimension_semantics=("arbitrary",),
                                             vmem_limit_bytes=VMEM_LIMIT),
        name="merge",
    )(oa, ob, sza, szb, ga, gb, x.reshape(rows, d_model), wsb_t, wmb_t, wout_t)
    return y.reshape(n_b, seq, d_model)


def _sb_decode_kernel(pt_ref, q_ref, k1_ref, v1_ref, k0_ref, v0_ref, ut_ref, kpool_ref, vpool_ref,
                      o_ref, acc_ref, carry_ref, kbuf, vbuf, sem, *, n_pages):
    b = pl.program_id(0)
    qmat, head_mask = _head_rows(q_ref[0])
    acc_ref[...] = jnp.zeros_like(acc_ref)
    carry_ref[...] = jnp.zeros_like(carry_ref)
    page_sz = ut_ref.shape[0]

    def process(kt, vt):
        z = jnp.dot(qmat, kt.astype(BF16), preferred_element_type=F32)
        ls, lk = _log_sigmoids(z)
        ut = ut_ref[...]
        cum = sum(jnp.dot(part, ut, preferred_element_type=F32) for part in _split3(lk))
        w = jnp.exp(ls + cum[:, :page_sz] + carry_ref[...])
        acc_ref[...] += lax.dot_general(w.astype(BF16), vt.astype(BF16), NT,
                                        preferred_element_type=F32)
        carry_ref[...] += cum[:, page_sz:]

    process(k1_ref[0], v1_ref[0])
    process(k0_ref[0], v0_ref[0])

    def cond(s):
        return jnp.logical_and(s[0] >= 0, s[1] >= SB_ZERO_LOG)

    def body(s):
        page = pt_ref[b * n_pages + s[0]]
        ck = pltpu.make_async_copy(kpool_ref.at[page], kbuf, sem.at[0])
        cv = pltpu.make_async_copy(vpool_ref.at[page], vbuf, sem.at[1])
        ck.start()
        cv.start()
        ck.wait()
        cv.wait()
        process(kbuf[...], vbuf[...])
        return s[0] - 1, jnp.max(carry_ref[...])

    lax.while_loop(cond, body, (n_pages - 3, jnp.max(carry_ref[...])))
    o_ref[0] = jnp.sum(jnp.where(head_mask, acc_ref[...], 0.0), axis=0, keepdims=True)


def _sb_decode(q_row, kpool, vpool, pt_flat, n_pages):
    n_seq = q_row.shape[0]
    page_sz = kpool.shape[-1]
    assert n_pages >= 2
    r = lax.broadcasted_iota(jnp.int32, (page_sz, 2 * page_sz), 0)
    c = lax.broadcasted_iota(jnp.int32, (page_sz, 2 * page_sz), 1)
    ut = jnp.where(jnp.logical_or(r > c, c >= page_sz), 1.0, 0.0).astype(BF16)
    page_spec = lambda back: pl.BlockSpec(
        (1, WIDTH, page_sz), lambda b, pt: (pt[b * n_pages + n_pages - back], 0, 0))
    grid_spec = pltpu.PrefetchScalarGridSpec(
        num_scalar_prefetch=1,
        grid=(n_seq,),
        in_specs=[pl.BlockSpec((1, 1, WIDTH), lambda b, pt: (b, 0, 0)),
                  page_spec(1), page_spec(1), page_spec(2), page_spec(2),
                  pl.BlockSpec((page_sz, 2 * page_sz), lambda b, pt: (0, 0)),
                  pl.BlockSpec(memory_space=pl.ANY), pl.BlockSpec(memory_space=pl.ANY)],
        out_specs=pl.BlockSpec((1, 1, WIDTH), lambda b, pt: (b, 0, 0)),
        scratch_shapes=[pltpu.VMEM((N_HEADS, WIDTH), F32), pltpu.VMEM((N_HEADS, page_sz), F32),
                        pltpu.VMEM((WIDTH, page_sz), F32), pltpu.VMEM((WIDTH, page_sz), F32),
                        pltpu.SemaphoreType.DMA((2,))])
    return pl.pallas_call(
        functools.partial(_sb_decode_kernel, n_pages=n_pages),
        grid_spec=grid_spec,
        out_shape=jax.ShapeDtypeStruct((n_seq, 1, WIDTH), F32),
        compiler_params=pltpu.CompilerParams(dimension_semantics=("arbitrary",),
                                             vmem_limit_bytes=VMEM_LIMIT),
        name="stickbreak_decode",
    )(pt_flat, q_row.reshape(n_seq, 1, WIDTH), kpool, vpool, kpool, vpool, ut, kpool, vpool)


def _mb_scores_kernel(pt_ref, q_ref, knew_ref, kpool_ref, z_ref, sel_ref, buf, sem, gate_ref,
                      *, n_pages, chunk):
    b = pl.program_id(0)
    c = pl.program_id(1)
    n_chunks = pl.num_programs(1)
    t = b * n_chunks + c
    total = pl.num_programs(0) * n_chunks
    page_sz = buf.shape[-1]

    def copies(tt, slot):
        base = (tt // n_chunks) * n_pages + (tt % n_chunks) * chunk
        return [pltpu.make_async_copy(kpool_ref.at[pt_ref[base + i]], buf.at[slot, i], sem.at[slot])
                for i in range(chunk)]

    @pl.when(t == 0)
    def _():
        for cp in copies(t, 0):
            cp.start()

    @pl.when(t + 1 < total)
    def _():
        for cp in copies(t + 1, (t + 1) % 2):
            cp.start()

    slot = t % 2
    for cp in copies(t, slot):
        cp.wait()

    qmat, _ = _head_rows(q_ref[0])
    lane = lax.broadcasted_iota(jnp.int32, (N_HEADS, page_sz), 1)

    @pl.when(c == 0)
    def _():
        gate_ref[...] = jnp.full(gate_ref.shape, -jnp.inf, F32)

    pages_per_blk = MB_BLOCK // page_sz
    gate = gate_ref[...]
    zsum = None
    for i in range(chunk):
        z = jnp.dot(qmat, buf[slot, i].astype(BF16), preferred_element_type=F32)
        z_ref[0, c * chunk + i] = z
        zsum = z if i % pages_per_blk == 0 else zsum + z
        if i % pages_per_blk == pages_per_blk - 1:
            blk_idx = c * (chunk // pages_per_blk) + i // pages_per_blk
            g = jnp.sum(zsum, axis=1, keepdims=True) * (1.0 / MB_BLOCK)
            gate = jnp.where(lane == blk_idx, g, gate)
    gate_ref[...] = gate

    @pl.when(c == n_chunks - 1)
    def _():
        knew = jnp.broadcast_to(knew_ref[0], (page_sz, WIDTH))
        z_ref[0, n_pages] = lax.dot_general(qmat, knew, NT, preferred_element_type=F32)
        gm = gate_ref[...]
        lane_f = lane.astype(F32)
        selv = jnp.zeros((N_HEADS, page_sz), jnp.int32)
        for it in range(MB_TOPK):
            m = jnp.max(gm, axis=1, keepdims=True)
            idx = jnp.min(jnp.where(gm == m, lane_f, 1e9), axis=1, keepdims=True)
            selv = jnp.where(lane == it, idx.astype(jnp.int32), selv)
            gm = jnp.where(lane_f == idx, -jnp.inf, gm)
        sel_ref[0] = selv


def _mb_scores(q_row, knew_row, kpool, pt_flat, n_pages):
    n_seq = q_row.shape[0]
    page_sz = kpool.shape[-1]
    chunk = min(GATE_CHUNK_PAGES, n_pages)
    assert n_pages % chunk == 0 and chunk % (MB_BLOCK // page_sz) == 0
    assert n_pages * page_sz // MB_BLOCK <= page_sz
    grid_spec = pltpu.PrefetchScalarGridSpec(
        num_scalar_prefetch=1,
        grid=(n_seq, n_pages // chunk),
        in_specs=[pl.BlockSpec((1, 1, WIDTH), lambda b, c, pt: (b, 0, 0)),
                  pl.BlockSpec((1, 1, WIDTH), lambda b, c, pt: (b, 0, 0)),
                  pl.BlockSpec(memory_space=pl.ANY)],
        out_specs=[pl.BlockSpec((1, n_pages + 1, N_HEADS, page_sz), lambda b, c, pt: (b, 0, 0, 0)),
                   pl.BlockSpec((1, N_HEADS, page_sz), lambda b, c, pt: (b, 0, 0))],
        scratch_shapes=[pltpu.VMEM((2, chunk, WIDTH, page_sz), F32),
                        pltpu.SemaphoreType.DMA((2,)),
                        pltpu.VMEM((N_HEADS, page_sz), F32)])
    return pl.pallas_call(
        functools.partial(_mb_scores_kernel, n_pages=n_pages, chunk=chunk),
        grid_spec=grid_spec,
        out_shape=[jax.ShapeDtypeStruct((n_seq, n_pages + 1, N_HEADS, page_sz), F32),
                   jax.ShapeDtypeStruct((n_seq, N_HEADS, page_sz), jnp.int32)],
        compiler_params=pltpu.CompilerParams(dimension_semantics=("arbitrary",) * 2,
                                             vmem_limit_bytes=VMEM_LIMIT),
        name="moba_decode_scores",
    )(pt_flat, q_row.reshape(n_seq, 1, WIDTH), knew_row.reshape(n_seq, 1, WIDTH), kpool)


def _mb_decode_kernel(pt_ref, sel_ref, z_ref, vnew_ref, vpool_ref, o_ref, vbuf, sem, *, n_pages):
    b = pl.program_id(0)
    n_seq = pl.num_programs(0)
    page_sz = vbuf.shape[-1]
    pages_per_blk = MB_BLOCK // page_sz
    n_chunks = MB_TOPK * pages_per_blk

    def sel_page(bb, h, ci):
        blk = sel_ref[(bb * N_HEADS + h) * MB_TOPK + ci // pages_per_blk]
        return blk * pages_per_blk + ci % pages_per_blk

    def copies(bb, slot):
        return [pltpu.make_async_copy(
                    vpool_ref.at[pt_ref[bb * n_pages + sel_page(bb, h, ci)], pl.ds(h * HEAD_DIM, HEAD_DIM), :],
                    vbuf.at[slot, h, ci], sem.at[slot])
                for h in range(N_HEADS) for ci in range(n_chunks)]

    @pl.when(b == 0)
    def _():
        for cp in copies(b, 0):
            cp.start()

    @pl.when(b + 1 < n_seq)
    def _():
        for cp in copies(b + 1, (b + 1) % 2):
            cp.start()

    slot = b % 2
    for cp in copies(b, slot):
        cp.wait()

    row = lax.broadcasted_iota(jnp.int32, (N_HEADS, page_sz), 0)
    chunks = []
    for ci in range(n_chunks):
        sc = jnp.zeros((N_HEADS, page_sz), F32)
        for h in range(N_HEADS):
            sc = jnp.where(row == h, z_ref[0, sel_page(b, h, ci)], sc)
        chunks.append(sc)
    s = jnp.concatenate(chunks, axis=1)
    s_own = z_ref[0, n_pages]
    m = jnp.maximum(jnp.max(s, axis=1, keepdims=True), s_own[:, :1])
    p = jnp.exp(s - m)
    p_own = jnp.exp(s_own[:, :1] - m)
    denom = jnp.sum(p, axis=1, keepdims=True) + p_own
    vsel = jnp.concatenate(
        [jnp.concatenate([vbuf[slot, h, ci] for ci in range(n_chunks)], axis=1) for h in range(N_HEADS)],
        axis=0)
    res = lax.dot_general(p.astype(BF16), vsel.astype(BF16), NT, preferred_element_type=F32)
    res = res + p_own.astype(BF16).astype(F32) * vnew_ref[0].astype(F32)
    head_mask = (lax.broadcasted_iota(jnp.int32, res.shape, 1) // HEAD_DIM
                 == lax.broadcasted_iota(jnp.int32, res.shape, 0))
    o_ref[0] = jnp.sum(jnp.where(head_mask, res / denom, 0.0), axis=0, keepdims=True)


def _mb_decode(z_all, sel_flat, vnew_row, vpool, pt_flat, n_pages):
    n_seq = z_all.shape[0]
    page_sz = vpool.shape[-1]
    n_chunks = MB_TOPK * (MB_BLOCK // page_sz)
    grid_spec = pltpu.PrefetchScalarGridSpec(
        num_scalar_prefetch=2,
        grid=(n_seq,),
        in_specs=[pl.BlockSpec((1, n_pages + 1, N_HEADS, page_sz), lambda b, pt, sel: (b, 0, 0, 0)),
                  pl.BlockSpec((1, 1, WIDTH), lambda b, pt, sel: (b, 0, 0)),
                  pl.BlockSpec(memory_space=pl.ANY)],
        out_specs=pl.BlockSpec((1, 1, WIDTH), lambda b, pt, sel: (b, 0, 0)),
        scratch_shapes=[pltpu.VMEM((2, N_HEADS, n_chunks, HEAD_DIM, page_sz), F32),
                        pltpu.SemaphoreType.DMA((2,))])
    return pl.pallas_call(
        functools.partial(_mb_decode_kernel, n_pages=n_pages),
        grid_spec=grid_spec,
        out_shape=jax.ShapeDtypeStruct((n_seq, 1, WIDTH), F32),
        compiler_params=pltpu.CompilerParams(dimension_semantics=("arbitrary",),
                                             vmem_limit_bytes=VMEM_LIMIT),
        name="moba_decode",
    )(pt_flat, sel_flat, z_all, vnew_row.reshape(n_seq, 1, WIDTH), vpool)


def _rope_tables(pos):
    inv = ROPE_THETA ** (-jnp.arange(HALF, dtype=F32) / HALF)
    ang = pos.astype(F32)[None, :] * inv[:, None]
    return jnp.cos(ang), jnp.sin(ang)


def _pool_pages(cache, layer):
    n_pool, slots = cache.shape[1], cache.shape[2]
    return jnp.transpose(cache[layer], (0, 2, 3, 1)).reshape(n_pool, WIDTH, slots)


def _heads_out(t):
    n_b, _, seq = t.shape
    return jnp.transpose(t.reshape(n_b, N_HEADS, HEAD_DIM, seq), (0, 3, 1, 2))


def kernel(x_prompt, x_sample, cache_sb_k, cache_sb_v, cache_mb_k, cache_mb_v, page_table,
           norm_gain, w_in, b_gate, sb_q_gain, sb_k_gain, mb_q_gain, mb_k_gain,
           w_branch_sb, w_branch_mb, w_out):
    depth = w_in.shape[0]
    n_seq, dec_seq, d_model = x_sample.shape
    assert dec_seq == 1
    seq = x_prompt.shape[1]
    n_pages = page_table.shape[1]
    page_sz = cache_sb_k.shape[2]
    past_len = n_pages * page_sz
    assert past_len % MB_BLOCK == 0 and past_len // MB_BLOCK >= MB_TOPK
    pt_flat = page_table.reshape(-1)
    cos_p, sin_p = _rope_tables(jnp.arange(seq))
    cos_s, sin_s = _rope_tables(jnp.full((n_seq,), past_len))

    xp, xs = x_prompt, x_sample.reshape(1, n_seq, d_model)
    prompt_kv = [[], [], [], []]
    sample_kv = [[], [], [], []]
    for l in range(depth):
        wt = w_in[l].T.astype(BF16)
        ng = norm_gain[l][None, :]
        bg = jnp.broadcast_to(b_gate[l][:, None], (2 * d_model, LANES))
        gains = jnp.stack([jnp.broadcast_to(jnp.tile(g[l], N_HEADS)[:, None], (WIDTH, LANES))
                           for g in (sb_q_gain, sb_k_gain, mb_q_gain, mb_k_gain)])
        wsb_t = w_branch_sb[l].T.astype(BF16)
        wmb_t = w_branch_mb[l].T.astype(BF16)
        wout_t = w_out[l].T.astype(BF16)

        (qa, ka, va, sza, qb, kb, vb, szb, ga, gb,
         ka_row, va_bf, kb_row, vb_bf, kmean) = _project(
            xp, cos_p, sin_p, ng, wt, bg, gains, tm=PROJ_ROWS, prompt=True)
        oa = _sb_prompt(qa, ka_row, va_bf, ATT_BLOCK)
        score_bound = SCALE * HEAD_DIM * jnp.max(jnp.abs(mb_q_gain[l])) * jnp.max(jnp.abs(mb_k_gain[l]))
        small_scores = (score_bound <= MB_SMALL_SCORE).astype(jnp.int32).reshape(1)
        ob = _mb_prompt(qb, kb_row, vb_bf, kmean, small_scores, MB_BLOCK)
        xp = _merge(xp, oa, ob, sza, szb, ga, gb, wsb_t, wmb_t, wout_t, MERGE_ROWS)
        for dst, t in zip(prompt_kv, (ka, va, kb, vb)):
            dst.append(_heads_out(t))

        (qa, ka, va, sza, qb, kb, vb, szb, ga, gb,
         qa_row, qb_row, kb_new, vb_new) = _project(
            xs, cos_s, sin_s, ng, wt, bg, gains, tm=n_seq, prompt=False)
        oa = _sb_decode(qa_row, _pool_pages(cache_sb_k, l), _pool_pages(cache_sb_v, l), pt_flat, n_pages)
        z_all, sel = _mb_scores(qb_row, kb_new, _pool_pages(cache_mb_k, l), pt_flat, n_pages)
        ob = _mb_decode(z_all, sel[:, :, :MB_TOPK].reshape(-1), vb_new,
                        _pool_pages(cache_mb_v, l), pt_flat, n_pages)
        to_fm = lambda o: jnp.transpose(o.reshape(n_seq, WIDTH))[None]
        xs = _merge(xs, to_fm(oa), to_fm(ob), sza, szb, ga, gb, wsb_t, wmb_t, wout_t, n_seq)
        for dst, t in zip(sample_kv, (ka, va, kb, vb)):
            dst.append(_heads_out(t).reshape(n_seq, 1, N_HEADS, HEAD_DIM))

    return (xp, xs.reshape(n_seq, 1, d_model),
            *[jnp.stack(t) for t in prompt_kv], *[jnp.stack(t) for t in sample_kv])
```

```python
import functools

import jax
import jax.numpy as jnp
from jax import lax
from jax.experimental import pallas as pl
from jax.experimental.pallas import tpu as pltpu

HEAD_DIM = 64
N_HEADS = 8
WIDTH = N_HEADS * HEAD_DIM
HALF = HEAD_DIM // 2
MB_BLOCK = 256
MB_TOPK = 3
ROPE_THETA = 10000.0
NORM_EPS = 1e-6
SCALE = HEAD_DIM ** -0.5

LANES = 128
ATT_BLOCK = 256
PROJ_ROWS = 256
MERGE_ROWS = 512
GATE_CHUNK_PAGES = 64
VMEM_LIMIT = 56 * 1024 * 1024

SB_ZERO_LOG = -104.0
NEG_BIG = -1e30
MB_SMALL_SCORE = 40.0
MB_GROUP = 4

F32 = jnp.float32
BF16 = jnp.bfloat16
NT = (((1,), (1,)), ((), ()))


def _tile_lanes(a, reps):
    return a if reps == 1 else jnp.concatenate([a] * reps, axis=1)


def _log_keep(z):
    return -(jnp.maximum(z, 0.0) + jnp.log(1.0 + jnp.exp(-jnp.abs(z))))


def _split2(x):
    hi = x.astype(BF16)
    return hi, (x - hi.astype(F32)).astype(BF16)


def _head_rows(q_row):
    shape = (N_HEADS, WIDTH)
    head_of_lane = lax.broadcasted_iota(jnp.int32, shape, 1) // HEAD_DIM
    mask = head_of_lane == lax.broadcasted_iota(jnp.int32, shape, 0)
    q32 = jnp.broadcast_to(q_row.astype(F32), shape)
    return jnp.where(mask, q32, 0.0).astype(q_row.dtype), mask


def _proj_kernel(x_ref, ng_ref, wt_ref, bg_ref, gains_ref, cos_ref, sin_ref, *outs, tm, prompt):
    (qa_ref, ka_ref, va_ref, sza_ref, qb_ref, kb_ref, vb_ref, szb_ref, ga_ref, gb_ref) = outs[:10]
    reps = tm // LANES
    x = x_ref[...]
    ms = jnp.mean(x * x, axis=-1, keepdims=True)
    h = (x * lax.rsqrt(ms + NORM_EPS) * ng_ref[...]).astype(BF16)

    def seg(j, rows=WIDTH):
        return lax.dot_general(wt_ref[j * WIDTH:j * WIDTH + rows, :], h, NT,
                               preferred_element_type=F32)

    def head_norm(p, gi):
        p3 = p.reshape(N_HEADS, HEAD_DIM, tm)
        hms = jnp.mean(p3 * p3, axis=1, keepdims=True)
        g = _tile_lanes(gains_ref[gi], reps).reshape(N_HEADS, HEAD_DIM, tm)
        return p3 * lax.rsqrt(hms + NORM_EPS) * g

    def rope(y3):
        c = cos_ref[...][None]
        s = sin_ref[...][None]
        x1, x2 = y3[:, :HALF], y3[:, HALF:]
        return jnp.concatenate([x1 * c - x2 * s, x2 * c + x1 * s], axis=1)

    def silu(z):
        return z * jax.nn.sigmoid(z)

    qa = (head_norm(seg(0), 0) * SCALE).reshape(WIDTH, tm)
    ka = head_norm(seg(1), 1).reshape(WIDTH, tm)
    va = seg(2)
    qb = (rope(head_norm(seg(4), 2)) * SCALE).reshape(WIDTH, tm)
    kb = rope(head_norm(seg(5), 3)).reshape(WIDTH, tm)
    vb = seg(6)
    qa_ref[0] = qa.astype(BF16)
    ka_ref[0] = ka
    va_ref[0] = va
    sza_ref[0] = silu(seg(3))
    qb_ref[0] = qb.astype(BF16)
    kb_ref[0] = kb
    vb_ref[0] = vb
    szb_ref[0] = silu(seg(7))
    d_model = ga_ref.shape[1]
    bg = _tile_lanes(bg_ref[...], reps)
    ga_ref[0] = jax.nn.sigmoid(seg(8, d_model) + bg[:d_model])
    gb_ref[0] = jax.nn.sigmoid(seg(8 + d_model // WIDTH, d_model) + bg[d_model:])
    if prompt:
        ka_row_ref, va_bf_ref, kb_row_ref, vb_bf_ref, kmean_ref = outs[10:]
        ka_row_ref[...] = ka.T.astype(BF16)
        va_bf_ref[0] = va.astype(BF16)
        kb_row = kb.T
        kb_row_ref[...] = kb_row.astype(BF16)
        vb_bf_ref[0] = vb.astype(BF16)
        kmean_ref[...] = jnp.broadcast_to(jnp.mean(kb_row, axis=0, keepdims=True), kmean_ref.shape)
    else:
        qa_row_ref, qb_row_ref, kb_row_ref, vb_row_ref = outs[10:]
        qa_row_ref[...] = qa.T.astype(BF16)
        qb_row_ref[...] = qb.T.astype(BF16)
        kb_row_ref[...] = kb.T.astype(BF16)
        vb_row_ref[...] = vb.T.astype(BF16)


def _project(x, cos_t, sin_t, ng, wt, bg, gains, *, tm, prompt):
    n_b, seq, d_model = x.shape
    rows = n_b * seq
    nt = seq // tm
    in_width = wt.shape[0]
    x2 = x.reshape(rows, d_model)
    fm = lambda f, dt: jax.ShapeDtypeStruct((n_b, f, seq), dt)
    fm_spec = lambda f: pl.BlockSpec((1, f, tm), lambda i: (i // nt, 0, i % nt))
    row_spec = pl.BlockSpec((tm, WIDTH), lambda i: (i, 0))
    const = lambda shape: pl.BlockSpec(shape, lambda i: (0,) * len(shape))
    out_shape = [fm(WIDTH, BF16), fm(WIDTH, F32), fm(WIDTH, F32), fm(WIDTH, F32),
                 fm(WIDTH, BF16), fm(WIDTH, F32), fm(WIDTH, F32), fm(WIDTH, F32),
                 fm(d_model, F32), fm(d_model, F32)]
    out_specs = [fm_spec(WIDTH)] * 8 + [fm_spec(d_model)] * 2
    row_bf = jax.ShapeDtypeStruct((rows, WIDTH), BF16)
    if prompt:
        assert tm == MB_BLOCK
        out_shape += [row_bf, fm(WIDTH, BF16), row_bf, fm(WIDTH, BF16),
                      jax.ShapeDtypeStruct((rows // tm * 8, WIDTH), F32)]
        out_specs += [row_spec, fm_spec(WIDTH), row_spec, fm_spec(WIDTH),
                      pl.BlockSpec((8, WIDTH), lambda i: (i, 0))]
    else:
        out_shape += [row_bf] * 4
        out_specs += [row_spec] * 4
    return pl.pallas_call(
        functools.partial(_proj_kernel, tm=tm, prompt=prompt),
        grid=(rows // tm,),
        in_specs=[pl.BlockSpec((tm, d_model), lambda i: (i, 0)),
                  const((1, d_model)),
                  const((in_width, d_model)),
                  const((2 * d_model, LANES)),
                  const((4, WIDTH, LANES)),
                  pl.BlockSpec((HALF, tm), lambda i: (0, i % nt)),
                  pl.BlockSpec((HALF, tm), lambda i: (0, i % nt))],
        out_specs=out_specs,
        out_shape=out_shape,
        compiler_params=pltpu.CompilerParams(dimension_semantics=("arbitrary",),
                                             vmem_limit_bytes=VMEM_LIMIT),
        name="project_prompt" if prompt else "project_sample",
    )(x2, ng, wt, bg, gains, cos_t, sin_t)


def _stack_heads(qt):
    row = lax.broadcasted_iota(jnp.int32, qt.shape, 0)
    zero = jnp.zeros_like(qt)
    return jnp.concatenate([jnp.where(row < HEAD_DIM, qt, zero),
                            jnp.where(row >= HEAD_DIM, qt, zero)], axis=1)


def _sb_prompt_kernel(qt_ref, k_ref, vt_ref, u_ref, ot_ref, acc_ref, carry_ref, *, blk):
    qi = pl.program_id(2)
    qs = _stack_heads(qt_ref[0])
    shape = (blk, blk)
    past = lax.broadcasted_iota(jnp.int32, shape, 0) < lax.broadcasted_iota(jnp.int32, shape, 1)
    acc_ref[...] = jnp.zeros_like(acc_ref)
    carry_ref[...] = jnp.zeros_like(carry_ref)

    def step(j, diag):
        start = pl.multiple_of(j * blk, blk)
        k_blk = k_ref[pl.ds(start, blk), :]
        vt_blk = vt_ref[0, :, pl.ds(start, blk)]
        carry = carry_ref[...]
        u = u_ref[...]
        halves = [slice(hd * blk, (hd + 1) * blk) for hd in range(2)]
        zs = [jnp.dot(k_blk, qs[:, cols], preferred_element_type=F32) for cols in halves]
        cums = []
        for z in zs:
            lk = _log_keep(z)
            if diag:
                lk = jnp.where(past, lk, 0.0)
            cums.append(jnp.dot(u, jnp.concatenate(_split2(lk), axis=0), preferred_element_type=F32))
        pv = []
        for hd, (z, cum, cols) in enumerate(zip(zs, cums, halves)):
            w = jnp.exp(z + cum + carry[:, cols])
            if diag:
                w = jnp.where(past, w, 0.0)
            pv.append(jnp.dot(vt_blk[hd * HEAD_DIM:(hd + 1) * HEAD_DIM], w.astype(BF16),
                              preferred_element_type=F32))
        acc_ref[...] += jnp.concatenate(pv, axis=1)
        carry_ref[...] = carry + jnp.concatenate([cum[0:1] for cum in cums], axis=1)

    step(qi, True)

    def cond(s):
        return jnp.logical_and(s[0] >= 0, s[1] >= SB_ZERO_LOG)

    def body(s):
        step(s[0], False)
        return s[0] - 1, jnp.max(carry_ref[...])

    lax.while_loop(cond, body, (qi - 1, jnp.max(carry_ref[...])))
    acc = acc_ref[...]
    ot_ref[0] = jnp.concatenate([acc[:, :blk], acc[:, blk:]], axis=0)


def _sb_prompt(qt, k_row, vt_bf, blk):
    n_b, _, seq = qt.shape
    n_blk = seq // blk
    pair = 2 * HEAD_DIM
    r = lax.broadcasted_iota(jnp.int32, (blk, 2 * blk), 0)
    c = lax.broadcasted_iota(jnp.int32, (blk, 2 * blk), 1) % blk
    u = jnp.where(c >= r, 1.0, 0.0).astype(BF16)
    return pl.pallas_call(
        functools.partial(_sb_prompt_kernel, blk=blk),
        grid=(n_b, N_HEADS // 2, n_blk),
        in_specs=[pl.BlockSpec((1, pair, blk), lambda b, hp, qi: (b, hp, qi)),
                  pl.BlockSpec((seq, pair), lambda b, hp, qi: (b, hp)),
                  pl.BlockSpec((1, pair, seq), lambda b, hp, qi: (b, hp, 0)),
                  pl.BlockSpec((blk, 2 * blk), lambda b, hp, qi: (0, 0))],
        out_specs=pl.BlockSpec((1, pair, blk), lambda b, hp, qi: (b, hp, qi)),
        out_shape=jax.ShapeDtypeStruct((n_b, WIDTH, seq), F32),
        scratch_shapes=[pltpu.VMEM((HEAD_DIM, 2 * blk), F32), pltpu.VMEM((1, 2 * blk), F32)],
        compiler_params=pltpu.CompilerParams(dimension_semantics=("arbitrary",) * 3,
                                             vmem_limit_bytes=VMEM_LIMIT),
        name="stickbreak_prompt",
    )(qt, k_row, vt_bf, u)


def _top3_rows(gate, n_idx, valid):
    gm = jnp.where(valid, gate, -jnp.inf)
    n_f = n_idx.astype(F32)
    sel = jnp.zeros(gate.shape, jnp.bool_)
    for _ in range(MB_TOPK):
        m = jnp.max(gm, axis=0, keepdims=True)
        idx = jnp.min(jnp.where(gm == m, n_f, 1e9), axis=0, keepdims=True)
        pick = n_f == idx
        sel = jnp.logical_or(sel, pick)
        gm = jnp.where(pick, -jnp.inf, gm)
    return jnp.logical_and(sel, valid)


def _mb_prompt_kernel(small_ref, qt_ref, k_ref, vt_ref, kmean_ref, ot_ref, acc_ref, m_ref, sel_ref,
                      *, blk, n_blk, grp):
    qi = pl.program_id(2)
    qs = _stack_heads(qt_ref[0])
    ones = jnp.ones((16, blk), BF16)

    km = kmean_ref[...].reshape(n_blk, 8, 2 * HEAD_DIM)[:, 0, :].astype(BF16)
    gate = jnp.dot(km, qs, preferred_element_type=F32)
    n_idx = lax.broadcasted_iota(jnp.int32, gate.shape, 0)
    sel = _top3_rows(gate, n_idx, n_idx < qi)

    def weighted_values(vt, ones_rows, p):
        return jnp.concatenate(
            [jnp.dot(jnp.concatenate([vt[hd * HEAD_DIM:(hd + 1) * HEAD_DIM], ones_rows], axis=0),
                     p[:, hd * blk:(hd + 1) * blk], preferred_element_type=F32) for hd in range(2)], axis=1)

    def scores(j):
        start = pl.multiple_of(j * blk, blk)
        s = jnp.dot(k_ref[pl.ds(start, blk), :], qs, preferred_element_type=F32)
        return s, vt_ref[0, :, pl.ds(start, blk)]

    s_own, vt_own = scores(qi)
    shape = (blk, 2 * blk)
    causal = lax.broadcasted_iota(jnp.int32, shape, 0) <= lax.broadcasted_iota(jnp.int32, shape, 1) % blk
    s_own = jnp.where(causal, s_own, -jnp.inf)
    m0 = jnp.max(s_own, axis=0, keepdims=True)
    acc0 = weighted_values(vt_own, ones, jnp.exp(s_own - m0).astype(BF16))

    @pl.when(small_ref[0] == 1)
    def _():
        sel_ref[...] = jnp.where(sel, 0.0, NEG_BIG)
        acc_ref[...] = acc0 * jnp.exp(m0)
        ones_g = jnp.ones((16, grp * blk), BF16)

        def body(g, carry):
            start = pl.multiple_of(g * (grp * blk), grp * blk)
            s = jnp.dot(k_ref[pl.ds(start, grp * blk), :], qs, preferred_element_type=F32)
            p = jnp.concatenate(
                [jnp.exp(s[i * blk:(i + 1) * blk] + sel_ref[pl.ds(g * grp + i, 1), :]) for i in range(grp)],
                axis=0).astype(BF16)
            acc_ref[...] += weighted_values(vt_ref[0, :, pl.ds(start, grp * blk)], ones_g, p)
            return carry

        lax.fori_loop(0, (qi + grp - 1) // grp, body, 0)

    @pl.when(small_ref[0] != 1)
    def _():
        sel_ref[...] = jnp.where(sel, 1.0, 0.0)
        m_ref[...] = m0
        acc_ref[...] = acc0

        def body(j, carry):
            s, vt = scores(j)
            is_sel = sel_ref[pl.ds(j, 1), :] > 0.5
            m_old = m_ref[...]
            m_new = jnp.where(is_sel, jnp.maximum(m_old, jnp.max(s, axis=0, keepdims=True)), m_old)
            p = jnp.exp(s - jnp.where(is_sel, m_new, jnp.inf))
            acc_ref[...] = acc_ref[...] * jnp.exp(m_old - m_new) + weighted_values(vt, ones, p.astype(BF16))
            m_ref[...] = m_new
            return carry

        lax.fori_loop(0, qi, body, 0)

    acc = acc_ref[...]
    out = acc[:HEAD_DIM] / acc[HEAD_DIM:HEAD_DIM + 1]
    ot_ref[0] = jnp.concatenate([out[:, :blk], out[:, blk:]], axis=0)


def _mb_prompt(qt, k_row, vt_bf, kmean, small_scores, blk):
    n_b, _, seq = qt.shape
    n_blk = seq // blk
    pair = 2 * HEAD_DIM
    grp = MB_GROUP if n_blk % MB_GROUP == 0 else 1
    grid_spec = pltpu.PrefetchScalarGridSpec(
        num_scalar_prefetch=1,
        grid=(n_b, N_HEADS // 2, n_blk),
        in_specs=[pl.BlockSpec((1, pair, blk), lambda b, hp, qi, f: (b, hp, qi)),
                  pl.BlockSpec((seq, pair), lambda b, hp, qi, f: (b, hp)),
                  pl.BlockSpec((1, pair, seq), lambda b, hp, qi, f: (b, hp, 0)),
                  pl.BlockSpec((n_blk * 8, pair), lambda b, hp, qi, f: (b, hp))],
        out_specs=pl.BlockSpec((1, pair, blk), lambda b, hp, qi, f: (b, hp, qi)),
        scratch_shapes=[pltpu.VMEM((HEAD_DIM + 16, 2 * blk), F32), pltpu.VMEM((1, 2 * blk), F32),
                        pltpu.VMEM((n_blk, 2 * blk), F32)])
    return pl.pallas_call(
        functools.partial(_mb_prompt_kernel, blk=blk, n_blk=n_blk, grp=grp),
        grid_spec=grid_spec,
        out_shape=jax.ShapeDtypeStruct((n_b, WIDTH, seq), F32),
        compiler_params=pltpu.CompilerParams(dimension_semantics=("arbitrary",) * 3,
                                             vmem_limit_bytes=VMEM_LIMIT),
        name="moba_prompt",
    )(small_scores, qt, k_row, vt_bf, kmean)


def _merge_kernel(oa_ref, ob_ref, sza_ref, szb_ref, ga_ref, gb_ref, x_ref,
                  wsb_ref, wmb_ref, wout_ref, y_ref):
    ua = (oa_ref[0] * sza_ref[0]).astype(BF16)
    ub = (ob_ref[0] * szb_ref[0]).astype(BF16)
    ya = jnp.dot(wsb_ref[...], ua, preferred_element_type=F32)
    yb = jnp.dot(wmb_ref[...], ub, preferred_element_type=F32)
    mix = (ga_ref[0] * ya + gb_ref[0] * yb).astype(BF16)
    yo = jnp.dot(wout_ref[...], mix, preferred_element_type=F32)
    y_ref[...] = x_ref[...] + yo.T


def _merge(x, oa, ob, sza, szb, ga, gb, wsb_t, wmb_t, wout_t, tm):
    n_b, seq, d_model = x.shape
    rows = n_b * seq
    nt = seq // tm
    fm_spec = lambda f: pl.BlockSpec((1, f, tm), lambda i: (i // nt, 0, i % nt))
    const = lambda shape: pl.BlockSpec(shape, lambda i: (0,) * len(shape))
    y = pl.pallas_call(
        _merge_kernel,
        grid=(rows // tm,),
        in_specs=[fm_spec(WIDTH)] * 4 + [fm_spec(d_model)] * 2
                 + [pl.BlockSpec((tm, d_model), lambda i: (i, 0)),
                    const((d_model, WIDTH)), const((d_model, WIDTH)), const((d_model, d_model))],
        out_specs=pl.BlockSpec((tm, d_model), lambda i: (i, 0)),
        out_shape=jax.ShapeDtypeStruct((rows, d_model), F32),
        compiler_params=pltpu.CompilerParams(dimension_semantics=("arbitrary",),
                                             vmem_limit_bytes=VMEM_LIMIT),
        name="merge",
    )(oa, ob, sza, szb, ga, gb, x.reshape(rows, d_model), wsb_t, wmb_t, wout_t)
    return y.reshape(n_b, seq, d_model)


def _sb_decode_kernel(pt_ref, q_ref, k1_ref, v1_ref, k0_ref, v0_ref, ut_ref, kpool_ref, vpool_ref,
                      o_ref, acc_ref, carry_ref, kbuf, vbuf, sem, *, n_pages):
    b = pl.program_id(0)
    qmat, head_mask = _head_rows(q_ref[0])
    acc_ref[...] = jnp.zeros_like(acc_ref)
    carry_ref[...] = jnp.zeros_like(carry_ref)
    span = ut_ref.shape[0] // 2

    def process(k_old, k_new, v_old, v_new):
        kt = jnp.concatenate([k_old, k_new], axis=1).astype(BF16)
        vt = jnp.concatenate([v_old, v_new], axis=1).astype(BF16)
        z = jnp.dot(qmat, kt, preferred_element_type=F32)
        cum = jnp.dot(jnp.concatenate(_split2(_log_keep(z)), axis=1), ut_ref[...],
                      preferred_element_type=F32)
        carry = carry_ref[...]
        w = jnp.exp(z + cum[:, :span] + _tile_lanes(carry, span // carry.shape[1]))
        acc_ref[...] += lax.dot_general(w.astype(BF16), vt, NT, preferred_element_type=F32)
        carry_ref[...] = carry + cum[:, span:]

    process(k0_ref[0], k1_ref[0], v0_ref[0], v1_ref[0])

    def cond(s):
        return jnp.logical_and(s[0] >= 1, s[1] >= SB_ZERO_LOG)

    def body(s):
        copies = []
        for i in range(2):
            page = pt_ref[b * n_pages + s[0] - 1 + i]
            copies.append(pltpu.make_async_copy(kpool_ref.at[page], kbuf.at[i], sem.at[0, i]))
            copies.append(pltpu.make_async_copy(vpool_ref.at[page], vbuf.at[i], sem.at[1, i]))
        for cp in copies:
            cp.start()
        for cp in copies:
            cp.wait()
        process(kbuf[0], kbuf[1], vbuf[0], vbuf[1])
        return s[0] - 2, jnp.max(carry_ref[...])

    lax.while_loop(cond, body, (n_pages - 3, jnp.max(carry_ref[...])))
    o_ref[0] = jnp.sum(jnp.where(head_mask, acc_ref[...], 0.0), axis=0, keepdims=True)


def _sb_decode(q_row, kpool, vpool, pt_flat, n_pages):
    n_seq = q_row.shape[0]
    page_sz = kpool.shape[-1]
    assert n_pages >= 2 and n_pages % 2 == 0
    span = 2 * page_sz
    r = lax.broadcasted_iota(jnp.int32, (2 * span, span + LANES), 0) % span
    c = lax.broadcasted_iota(jnp.int32, (2 * span, span + LANES), 1)
    ut = jnp.where(jnp.logical_or(r >= c, c >= span), 1.0, 0.0).astype(BF16)
    page_spec = lambda back: pl.BlockSpec(
        (1, WIDTH, page_sz), lambda b, pt: (pt[b * n_pages + n_pages - back], 0, 0))
    grid_spec = pltpu.PrefetchScalarGridSpec(
        num_scalar_prefetch=1,
        grid=(n_seq,),
        in_specs=[pl.BlockSpec((1, 1, WIDTH), lambda b, pt: (b, 0, 0)),
                  page_spec(1), page_spec(1), page_spec(2), page_spec(2),
                  pl.BlockSpec((2 * span, span + LANES), lambda b, pt: (0, 0)),
                  pl.BlockSpec(memory_space=pl.ANY), pl.BlockSpec(memory_space=pl.ANY)],
        out_specs=pl.BlockSpec((1, 1, WIDTH), lambda b, pt: (b, 0, 0)),
        scratch_shapes=[pltpu.VMEM((N_HEADS, WIDTH), F32), pltpu.VMEM((N_HEADS, LANES), F32),
                        pltpu.VMEM((2, WIDTH, page_sz), F32), pltpu.VMEM((2, WIDTH, page_sz), F32),
                        pltpu.SemaphoreType.DMA((2, 2))])
    return pl.pallas_call(
        functools.partial(_sb_decode_kernel, n_pages=n_pages),
        grid_spec=grid_spec,
        out_shape=jax.ShapeDtypeStruct((n_seq, 1, WIDTH), F32),
        compiler_params=pltpu.CompilerParams(dimension_semantics=("arbitrary",),
                                             vmem_limit_bytes=VMEM_LIMIT),
        name="stickbreak_decode",
    )(pt_flat, q_row.reshape(n_seq, 1, WIDTH), kpool, vpool, kpool, vpool, ut, kpool, vpool)


def _mb_scores_kernel(pt_ref, q_ref, knew_ref, kpool_ref, z_ref, sel_ref, buf, sem, gate_ref,
                      *, n_pages, chunk):
    b = pl.program_id(0)
    c = pl.program_id(1)
    n_chunks = pl.num_programs(1)
    t = b * n_chunks + c
    total = pl.num_programs(0) * n_chunks
    page_sz = buf.shape[-1]

    def copies(tt, slot):
        base = (tt // n_chunks) * n_pages + (tt % n_chunks) * chunk
        return [pltpu.make_async_copy(kpool_ref.at[pt_ref[base + i]], buf.at[slot, i], sem.at[slot])
                for i in range(chunk)]

    @pl.when(t == 0)
    def _():
        for cp in copies(t, 0):
            cp.start()

    @pl.when(t + 1 < total)
    def _():
        for cp in copies(t + 1, (t + 1) % 2):
            cp.start()

    slot = t % 2
    for cp in copies(t, slot):
        cp.wait()

    qmat, _ = _head_rows(q_ref[0])
    lane = lax.broadcasted_iota(jnp.int32, (N_HEADS, page_sz), 1)

    @pl.when(c == 0)
    def _():
        gate_ref[...] = jnp.full(gate_ref.shape, -jnp.inf, F32)

    pages_per_blk = MB_BLOCK // page_sz
    gate = gate_ref[...]
    zsum = None
    for i in range(chunk):
        z = jnp.dot(qmat, buf[slot, i].astype(BF16), preferred_element_type=F32)
        z_ref[0, c * chunk + i] = z
        zsum = z if i % pages_per_blk == 0 else zsum + z
        if i % pages_per_blk == pages_per_blk - 1:
            blk_idx = c * (chunk // pages_per_blk) + i // pages_per_blk
            g = jnp.sum(zsum, axis=1, keepdims=True) * (1.0 / MB_BLOCK)
            gate = jnp.where(lane == blk_idx, g, gate)
    gate_ref[...] = gate

    @pl.when(c == n_chunks - 1)
    def _():
        knew = jnp.broadcast_to(knew_ref[0], (page_sz, WIDTH))
        z_ref[0, n_pages] = lax.dot_general(qmat, knew, NT, preferred_element_type=F32)
        gm = gate_ref[...]
        lane_f = lane.astype(F32)
        selv = jnp.zeros((N_HEADS, page_sz), jnp.int32)
        for it in range(MB_TOPK):
            m = jnp.max(gm, axis=1, keepdims=True)
            idx = jnp.min(jnp.where(gm == m, lane_f, 1e9), axis=1, keepdims=True)
            selv = jnp.where(lane == it, idx.astype(jnp.int32), selv)
            gm = jnp.where(lane_f == idx, -jnp.inf, gm)
        sel_ref[0] = selv


def _mb_scores(q_row, knew_row, kpool, pt_flat, n_pages):
    n_seq = q_row.shape[0]
    page_sz = kpool.shape[-1]
    chunk = min(GATE_CHUNK_PAGES, n_pages)
    assert n_pages % chunk == 0 and chunk % (MB_BLOCK // page_sz) == 0
    assert n_pages * page_sz // MB_BLOCK <= page_sz
    grid_spec = pltpu.PrefetchScalarGridSpec(
        num_scalar_prefetch=1,
        grid=(n_seq, n_pages // chunk),
        in_specs=[pl.BlockSpec((1, 1, WIDTH), lambda b, c, pt: (b, 0, 0)),
                  pl.BlockSpec((1, 1, WIDTH), lambda b, c, pt: (b, 0, 0)),
                  pl.BlockSpec(memory_space=pl.ANY)],
        out_specs=[pl.BlockSpec((1, n_pages + 1, N_HEADS, page_sz), lambda b, c, pt: (b, 0, 0, 0)),
                   pl.BlockSpec((1, N_HEADS, page_sz), lambda b, c, pt: (b, 0, 0))],
        scratch_shapes=[pltpu.VMEM((2, chunk, WIDTH, page_sz), F32),
                        pltpu.SemaphoreType.DMA((2,)),
                        pltpu.VMEM((N_HEADS, page_sz), F32)])
    return pl.pallas_call(
        functools.partial(_mb_scores_kernel, n_pages=n_pages, chunk=chunk),
        grid_spec=grid_spec,
        out_shape=[jax.ShapeDtypeStruct((n_seq, n_pages + 1, N_HEADS, page_sz), F32),
                   jax.ShapeDtypeStruct((n_seq, N_HEADS, page_sz), jnp.int32)],
        compiler_params=pltpu.CompilerParams(dimension_semantics=("arbitrary",) * 2,
                                             vmem_limit_bytes=VMEM_LIMIT),
        name="moba_decode_scores",
    )(pt_flat, q_row.reshape(n_seq, 1, WIDTH), knew_row.reshape(n_seq, 1, WIDTH), kpool)


def _mb_decode_kernel(pt_ref, sel_ref, z_ref, vnew_ref, vpool_ref, o_ref, vbuf, sem, *, n_pages):
    b = pl.program_id(0)
    n_seq = pl.num_programs(0)
    page_sz = vbuf.shape[-1]
    pages_per_blk = MB_BLOCK // page_sz
    n_chunks = MB_TOPK * pages_per_blk

    def sel_page(bb, h, ci):
        blk = sel_ref[(bb * N_HEADS + h) * MB_TOPK + ci // pages_per_blk]
        return blk * pages_per_blk + ci % pages_per_blk

    def copies(bb, slot):
        return [pltpu.make_async_copy(
                    vpool_ref.at[pt_ref[bb * n_pages + sel_page(bb, h, ci)], pl.ds(h * HEAD_DIM, HEAD_DIM), :],
                    vbuf.at[slot, h, ci], sem.at[slot])
                for h in range(N_HEADS) for ci in range(n_chunks)]

    @pl.when(b == 0)
    def _():
        for cp in copies(b, 0):
            cp.start()

    @pl.when(b + 1 < n_seq)
    def _():
        for cp in copies(b + 1, (b + 1) % 2):
            cp.start()

    slot = b % 2
    for cp in copies(b, slot):
        cp.wait()

    row = lax.broadcasted_iota(jnp.int32, (N_HEADS, page_sz), 0)
    chunks = []
    for ci in range(n_chunks):
        sc = jnp.zeros((N_HEADS, page_sz), F32)
        for h in range(N_HEADS):
            sc = jnp.where(row == h, z_ref[0, sel_page(b, h, ci)], sc)
        chunks.append(sc)
    s = jnp.concatenate(chunks, axis=1)
    s_own = z_ref[0, n_pages]
    m = jnp.maximum(jnp.max(s, axis=1, keepdims=True), s_own[:, :1])
    p = jnp.exp(s - m)
    p_own = jnp.exp(s_own[:, :1] - m)
    denom = jnp.sum(p, axis=1, keepdims=True) + p_own
    vsel = jnp.concatenate(
        [jnp.concatenate([vbuf[slot, h, ci] for ci in range(n_chunks)], axis=1) for h in range(N_HEADS)],
        axis=0)
    res = lax.dot_general(p.astype(BF16), vsel.astype(BF16), NT, preferred_element_type=F32)
    res = res + p_own.astype(BF16).astype(F32) * vnew_ref[0].astype(F32)
    head_mask = (lax.broadcasted_iota(jnp.int32, res.shape, 1) // HEAD_DIM
                 == lax.broadcasted_iota(jnp.int32, res.shape, 0))
    o_ref[0] = jnp.sum(jnp.where(head_mask, res / denom, 0.0), axis=0, keepdims=True)


def _mb_decode(z_all, sel_flat, vnew_row, vpool, pt_flat, n_pages):
    n_seq = z_all.shape[0]
    page_sz = vpool.shape[-1]
    n_chunks = MB_TOPK * (MB_BLOCK // page_sz)
    grid_spec = pltpu.PrefetchScalarGridSpec(
        num_scalar_prefetch=2,
        grid=(n_seq,),
        in_specs=[pl.BlockSpec((1, n_pages + 1, N_HEADS, page_sz), lambda b, pt, sel: (b, 0, 0, 0)),
                  pl.BlockSpec((1, 1, WIDTH), lambda b, pt, sel: (b, 0, 0)),
                  pl.BlockSpec(memory_space=pl.ANY)],
        out_specs=pl.BlockSpec((1, 1, WIDTH), lambda b, pt, sel: (b, 0, 0)),
        scratch_shapes=[pltpu.VMEM((2, N_HEADS, n_chunks, HEAD_DIM, page_sz), F32),
                        pltpu.SemaphoreType.DMA((2,))])
    return pl.pallas_call(
        functools.partial(_mb_decode_kernel, n_pages=n_pages),
        grid_spec=grid_spec,
        out_shape=jax.ShapeDtypeStruct((n_seq, 1, WIDTH), F32),
        compiler_params=pltpu.CompilerParams(dimension_semantics=("arbitrary",),
                                             vmem_limit_bytes=VMEM_LIMIT),
        name="moba_decode",
    )(pt_flat, sel_flat, z_all, vnew_row.reshape(n_seq, 1, WIDTH), vpool)


def _rope_tables(pos):
    inv = ROPE_THETA ** (-jnp.arange(HALF, dtype=F32) / HALF)
    ang = pos.astype(F32)[None, :] * inv[:, None]
    return jnp.cos(ang), jnp.sin(ang)


def _pool_pages(cache, layer):
    n_pool, slots = cache.shape[1], cache.shape[2]
    return jnp.transpose(cache[layer], (0, 2, 3, 1)).reshape(n_pool, WIDTH, slots)


def _heads_out(t):
    n_b, _, seq = t.shape
    return jnp.transpose(t.reshape(n_b, N_HEADS, HEAD_DIM, seq), (0, 3, 1, 2))


def kernel(x_prompt, x_sample, cache_sb_k, cache_sb_v, cache_mb_k, cache_mb_v, page_table,
           norm_gain, w_in, b_gate, sb_q_gain, sb_k_gain, mb_q_gain, mb_k_gain,
           w_branch_sb, w_branch_mb, w_out):
    depth = w_in.shape[0]
    n_seq, dec_seq, d_model = x_sample.shape
    assert dec_seq == 1
    seq = x_prompt.shape[1]
    n_pages = page_table.shape[1]
    page_sz = cache_sb_k.shape[2]
    past_len = n_pages * page_sz
    assert past_len % MB_BLOCK == 0 and past_len // MB_BLOCK >= MB_TOPK
    pt_flat = page_table.reshape(-1)
    cos_p, sin_p = _rope_tables(jnp.arange(seq))
    cos_s, sin_s = _rope_tables(jnp.full((n_seq,), past_len))

    xp, xs = x_prompt, x_sample.reshape(1, n_seq, d_model)
    prompt_kv = [[], [], [], []]
    sample_kv = [[], [], [], []]
    for l in range(depth):
        wt = w_in[l].T.astype(BF16)
        ng = norm_gain[l][None, :]
        bg = jnp.broadcast_to(b_gate[l][:, None], (2 * d_model, LANES))
        gains = jnp.stack([jnp.broadcast_to(jnp.tile(g[l], N_HEADS)[:, None], (WIDTH, LANES))
                           for g in (sb_q_gain, sb_k_gain, mb_q_gain, mb_k_gain)])
        wsb_t = w_branch_sb[l].T.astype(BF16)
        wmb_t = w_branch_mb[l].T.astype(BF16)
        wout_t = w_out[l].T.astype(BF16)

        (qa, ka, va, sza, qb, kb, vb, szb, ga, gb,
         ka_row, va_bf, kb_row, vb_bf, kmean) = _project(
            xp, cos_p, sin_p, ng, wt, bg, gains, tm=PROJ_ROWS, prompt=True)
        oa = _sb_prompt(qa, ka_row, va_bf, ATT_BLOCK)
        score_bound = SCALE * HEAD_DIM * jnp.max(jnp.abs(mb_q_gain[l])) * jnp.max(jnp.abs(mb_k_gain[l]))
        small_scores = (score_bound <= MB_SMALL_SCORE).astype(jnp.int32).reshape(1)
        ob = _mb_prompt(qb, kb_row, vb_bf, kmean, small_scores, MB_BLOCK)
        xp = _merge(xp, oa, ob, sza, szb, ga, gb, wsb_t, wmb_t, wout_t, MERGE_ROWS)
        for dst, t in zip(prompt_kv, (ka, va, kb, vb)):
            dst.append(_heads_out(t))

        (qa, ka, va, sza, qb, kb, vb, szb, ga, gb,
         qa_row, qb_row, kb_new, vb_new) = _project(
            xs, cos_s, sin_s, ng, wt, bg, gains, tm=n_seq, prompt=False)
        oa = _sb_decode(qa_row, _pool_pages(cache_sb_k, l), _pool_pages(cache_sb_v, l), pt_flat, n_pages)
        z_all, sel = _mb_scores(qb_row, kb_new, _pool_pages(cache_mb_k, l), pt_flat, n_pages)
        ob = _mb_decode(z_all, sel[:, :, :MB_TOPK].reshape(-1), vb_new,
                        _pool_pages(cache_mb_v, l), pt_flat, n_pages)
        to_fm = lambda o: jnp.transpose(o.reshape(n_seq, WIDTH))[None]
        xs = _merge(xs, to_fm(oa), to_fm(ob), sza, szb, ga, gb, wsb_t, wmb_t, wout_t, n_seq)
        for dst, t in zip(sample_kv, (ka, va, kb, vb)):
            dst.append(_heads_out(t).reshape(n_seq, 1, N_HEADS, HEAD_DIM))

    return (xp, xs.reshape(n_seq, 1, d_model),
            *[jnp.stack(t) for t in prompt_kv], *[jnp.stack(t) for t in sample_kv])
```
